```python
import jax, jax.numpy as jnp
from jax import lax
import numpy as np

D_MODEL = 2048
BATCH = 2
SEQ = 4096
DEPTH = 4
DEC_BATCH = 8
DEC_SEQ = 1
PAST_LEN = 16384
PAGE_SIZE = 128

D_POOL = D_MODEL // 2
POOL_WINDOWS = (2, 4, 8, 16)
N_POOL_GROUPS = len(POOL_WINDOWS)
POOL_GROUP = D_POOL // N_POOL_GROUPS
POOL_MAX = max(POOL_WINDOWS)
D_ATTN = D_MODEL - D_POOL
HEAD_DIM = 64
N_HEADS = D_ATTN // HEAD_DIM
DILATED_PATTERNS = ((128, 1), (512, 4), (2048, 16))
WIN_MAX = max(w for w, _ in DILATED_PATTERNS)
BAND_BLOCK = 128
D_IN = 2 * D_POOL + 4 * D_ATTN
RMS_EPS = 1e-6

kernel_name = 'hybrid_pool_dilated_swa_step'


def rms_norm(x, g):
    xf = x.astype(jnp.float32)
    y = xf * lax.rsqrt(jnp.mean(xf * xf, axis=-1, keepdims=True) + RMS_EPS)
    return (y * g.astype(jnp.float32)).astype(x.dtype)


def alibi_slopes():
    return jnp.asarray(2.0 ** (-8.0 * np.arange(1, N_HEADS + 1) / N_HEADS), dtype=jnp.float32)


def split_proj(h, w_in):
    B, T, _ = h.shape
    z = jnp.einsum('btd,de->bte', h, w_in)
    cuts = [D_POOL, 2 * D_POOL, 2 * D_POOL + D_ATTN, 2 * D_POOL + 2 * D_ATTN, 2 * D_POOL + 3 * D_ATTN]
    u, gp, q, k, v, ga = jnp.split(z, cuts, axis=-1)
    heads = lambda a: a.reshape(B, T, N_HEADS, HEAD_DIM)
    return u, gp, heads(q), heads(k), heads(v), ga


def pool_mix(u, buf, pos0, w_lin, scale):
    B, T, _ = u.shape
    ext = jnp.concatenate([buf.astype(u.dtype), u], axis=1)
    cs = jnp.pad(jnp.cumsum(ext.astype(jnp.float32), axis=1), ((0, 0), (1, 0), (0, 0)))
    hi = cs[:, POOL_MAX:]
    pos = pos0 + jnp.arange(T)
    means = []
    for g, w in enumerate(POOL_WINDOWS):
        sl = slice(g * POOL_GROUP, (g + 1) * POOL_GROUP)
        lo = cs[:, POOL_MAX - w:POOL_MAX - w + T, sl]
        cnt = jnp.minimum(w, pos + 1).astype(jnp.float32)[None, :, None]
        means.append((hi[..., sl] - lo) / cnt)
    pooled = jnp.concatenate(means, axis=-1) - u.astype(jnp.float32)
    pooled = pooled.reshape(B, T, N_POOL_GROUPS, POOL_GROUP)
    mixed = jnp.einsum('btgc,gcd->btgd', pooled, w_lin.astype(jnp.float32)).reshape(B, T, D_POOL)
    mixed = mixed * scale.astype(jnp.float32)
    return mixed, ext[:, -(POOL_MAX - 1):]


def combine_patterns(outs, lses):
    w = jax.nn.softmax(jnp.stack(lses, axis=0), axis=0)
    return jnp.sum(w[..., None] * jnp.stack(outs, axis=0), axis=0)


def dilated_attn_prompt(q, k, v, slopes):
    B, T, H, E = q.shape
    scale = HEAD_DIM ** -0.5
    outs, lses = [], []
    for window, d in DILATED_PATTERNS:
        n_steps = window // d
        n_sub = T // d
        nb = -(-n_sub // BAND_BLOCK)
        n_pad = nb * BAND_BLOCK

        def to_blocks(a):
            a = a.reshape(B, n_sub, d, H, E).transpose(0, 2, 1, 3, 4)
            a = jnp.pad(a.astype(jnp.float32), ((0, 0), (0, 0), (0, n_pad - n_sub), (0, 0), (0, 0)))
            return a.reshape(B, d, nb, BAND_BLOCK, H, E)

        qb, kb, vb = to_blocks(q), to_blocks(k), to_blocks(v)
        prev = lambda a: jnp.concatenate([jnp.zeros_like(a[:, :, :1]), a[:, :, :-1]], axis=2)
        kk = jnp.concatenate([prev(kb), kb], axis=3)
        vv = jnp.concatenate([prev(vb), vb], axis=3)
        s = jnp.einsum('brnqhe,brnkhe->brnhqk', qb, kk) * scale
        qi = jnp.arange(BAND_BLOCK)[:, None] + BAND_BLOCK
        kj = jnp.arange(2 * BAND_BLOCK)[None, :]
        dist = qi - kj
        key_sub = jnp.arange(nb)[:, None, None] * BAND_BLOCK - BAND_BLOCK + kj[None]
        valid = (dist >= 0) & (dist <= n_steps) & (key_sub >= 0)
        bias = -slopes[:, None, None] * (dist * d).astype(jnp.float32)
        s = jnp.where(valid[:, None], s + bias[None], -jnp.inf)
        m = jnp.max(s, axis=-1, keepdims=True)
        p = jnp.exp(s - m)
        den = jnp.sum(p, axis=-1)
        den_t = den.transpose(0, 1, 2, 4, 3)
        o = jnp.einsum('brnhqk,brnkhe->brnqhe', p, vv) / den_t[..., None]
        lse = m[..., 0].transpose(0, 1, 2, 4, 3) + jnp.log(den_t)

        def from_blocks(a):
            a = a.reshape(B, d, n_pad, *a.shape[4:])[:, :, :n_sub]
            a = jnp.moveaxis(a, 1, 2)
            return a.reshape(B, T, *a.shape[3:])

        outs.append(from_blocks(o))
        lses.append(from_blocks(lse))
    return combine_patterns(outs, lses)


def dilated_attn_sample(q, k, v, k_buf, v_buf, slopes):
    L = k_buf.shape[1]
    S = q.shape[1]
    scale = HEAD_DIM ** -0.5
    k_all = jnp.concatenate([k_buf.astype(jnp.float32), k.astype(jnp.float32)], axis=1)
    v_all = jnp.concatenate([v_buf.astype(jnp.float32), v.astype(jnp.float32)], axis=1)
    qf = q.astype(jnp.float32)
    c = L + jnp.arange(S)
    outs, lses = [], []
    for window, d in DILATED_PATTERNS:
        steps = jnp.arange(window // d + 1)
        idx = c[:, None] - steps[None, :] * d
        valid = idx >= 0
        idx = jnp.maximum(idx, 0)
        kg = k_all[:, idx]
        vg = v_all[:, idx]
        s = jnp.einsum('bshe,bsnhe->bhsn', qf, kg) * scale
        bias = -slopes[:, None, None] * (steps * d).astype(jnp.float32)[None, None, :]
        s = jnp.where(valid[None, None], s + bias, -jnp.inf)
        m = jnp.max(s, axis=-1, keepdims=True)
        p = jnp.exp(s - m)
        den = jnp.sum(p, axis=-1)
        den_t = den.transpose(0, 2, 1)
        o = jnp.einsum('bhsn,bsnhe->bshe', p, vg) / den_t[..., None]
        lse = m[..., 0].transpose(0, 2, 1) + jnp.log(den_t)
        outs.append(o)
        lses.append(lse)
    return combine_patterns(outs, lses)


def merge_out(pm, gp, a, ga, w_out, dtype):
    B, T, _ = pm.shape
    f32 = jnp.float32
    mix = jnp.concatenate([pm * jax.nn.silu(gp.astype(f32)),
                           a.reshape(B, T, D_ATTN) * jax.nn.silu(ga.astype(f32))], axis=-1).astype(dtype)
    return jnp.einsum('btm,md->btd', mix, w_out)


def setup_inputs(seed: int = 0) -> dict:
    key = jax.random.key(seed)
    ks = jax.random.split(key, 12)
    l_win = min(WIN_MAX, PAST_LEN)
    f32 = jnp.float32
    return {
        'x_prompt': jax.random.normal(ks[0], (BATCH, SEQ, D_MODEL), f32),
        'x_sample': jax.random.normal(ks[1], (DEC_BATCH, DEC_SEQ, D_MODEL), f32),
        'state_pool': jax.random.normal(ks[2], (DEPTH, DEC_BATCH, POOL_MAX - 1, D_POOL), f32),
        'cache_k_win': jax.random.normal(ks[3], (DEPTH, DEC_BATCH, l_win, N_HEADS, HEAD_DIM), f32),
        'cache_v_win': jax.random.normal(ks[4], (DEPTH, DEC_BATCH, l_win, N_HEADS, HEAD_DIM), f32),
        'norm_g': 1.0 + 0.05 * jax.random.normal(ks[5], (DEPTH, D_MODEL), f32),
        'w_in': jax.random.normal(ks[6], (DEPTH, D_MODEL, D_IN), f32) * D_MODEL ** -0.5,
        'pool_w': jax.random.normal(ks[7], (DEPTH, N_POOL_GROUPS, POOL_GROUP, POOL_GROUP), f32) * POOL_GROUP ** -0.5,
        'pool_scale': 1.0 + 0.1 * jax.random.normal(ks[8], (DEPTH, D_POOL), f32),
        'w_out': jax.random.normal(ks[9], (DEPTH, D_POOL + D_ATTN, D_MODEL), f32) * (D_POOL + D_ATTN) ** -0.5,
        'final_norm_g': 1.0 + 0.05 * jax.random.normal(ks[10], (D_MODEL,), f32),
    }


def reference(x_prompt, x_sample, state_pool, cache_k_win, cache_v_win,
              norm_g, w_in, pool_w, pool_scale, w_out, final_norm_g):
    slopes = alibi_slopes()

    B, T, _ = x_prompt.shape
    keep = min(WIN_MAX, T)
    xp = x_prompt
    pool_p, k_p, v_p = [], [], []
    for l in range(DEPTH):
        h = rms_norm(xp, norm_g[l])
        u, gp, q, k, v, ga = split_proj(h, w_in[l])
        zero_buf = jnp.zeros((B, POOL_MAX - 1, D_POOL), u.dtype)
        pm, pbuf = pool_mix(u, zero_buf, 0, pool_w[l], pool_scale[l])
        a = dilated_attn_prompt(q, k, v, slopes)
        xp = xp + merge_out(pm, gp, a, ga, w_out[l], xp.dtype)
        pool_p.append(pbuf)
        k_p.append(k[:, T - keep:])
        v_p.append(v[:, T - keep:])
    y_prompt = rms_norm(xp, final_norm_g)

    xs = x_sample
    pool_s, k_s, v_s = [], [], []
    for l in range(DEPTH):
        h = rms_norm(xs, norm_g[l])
        u, gp, q, k, v, ga = split_proj(h, w_in[l])
        pm, pbuf = pool_mix(u, state_pool[l], PAST_LEN, pool_w[l], pool_scale[l])
        a = dilated_attn_sample(q, k, v, cache_k_win[l], cache_v_win[l], slopes)
        xs = xs + merge_out(pm, gp, a, ga, w_out[l], xs.dtype)
        pool_s.append(pbuf)
        k_s.append(k)
        v_s.append(v)
    y_sample = rms_norm(xs, final_norm_g)

    return (y_prompt, y_sample,
            jnp.stack(pool_p), jnp.stack(k_p), jnp.stack(v_p),
            jnp.stack(pool_s), jnp.stack(k_s), jnp.stack(v_s))
```

```python
import functools

import numpy as np
import jax
import jax.numpy as jnp
from jax import lax
from jax.experimental import pallas as pl
from jax.experimental.pallas import tpu as pltpu

HEAD_DIM = 64
LANES = 128
HEADS_PER_TILE = LANES // HEAD_DIM
POOL_WINDOWS = (2, 4, 8, 16)
POOL_MAX = max(POOL_WINDOWS)
DILATED_PATTERNS = ((128, 1), (512, 4), (2048, 16))
BAND = 128
WIN_MAX = max(w for w, _ in DILATED_PATTERNS)
RMS_EPS = 1e-6
VMEM_LIMIT_BYTES = 56 * 2**20

F32 = jnp.float32
BF16 = jnp.bfloat16
NEG_INF = float("-inf")


def _alibi_slopes(n_heads):
    return (2.0 ** (-8.0 * np.arange(1, n_heads + 1) / n_heads)).astype(np.float32)


def _silu(x):
    return x * (1.0 / (1.0 + jnp.exp(-x)))


def _params(*sem):
    return pltpu.CompilerParams(dimension_semantics=sem, vmem_limit_bytes=VMEM_LIMIT_BYTES)


def _in_proj_kernel(x_ref, g_ref, w_ref, z_ref, h_ref):
    @pl.when(pl.program_id(1) == 0)
    def _():
        x = x_ref[...]
        y = x * lax.rsqrt(jnp.mean(x * x, axis=-1, keepdims=True) + RMS_EPS)
        h_ref[...] = (y * g_ref[...]).astype(BF16)

    z_ref[...] = jnp.dot(h_ref[...], w_ref[...], preferred_element_type=F32)


def _in_proj(x, g, w_bf16, *, tm, tn):
    m, d = x.shape
    n = w_bf16.shape[1]
    return pl.pallas_call(
        _in_proj_kernel,
        out_shape=jax.ShapeDtypeStruct((m, n), F32),
        grid=(m // tm, n // tn),
        in_specs=[
            pl.BlockSpec((tm, d), lambda i, j: (i, 0)),
            pl.BlockSpec((1, d), lambda i, j: (0, 0)),
            pl.BlockSpec((d, tn), lambda i, j: (0, j)),
        ],
        out_specs=pl.BlockSpec((tm, tn), lambda i, j: (i, j)),
        scratch_shapes=[pltpu.VMEM((tm, d), BF16)],
        compiler_params=_params("parallel", "arbitrary"),
        name="in_proj",
    )(x, g.reshape(1, d), w_bf16)


def _rms_norm_kernel(x_ref, g_ref, o_ref):
    x = x_ref[...]
    y = x * lax.rsqrt(jnp.mean(x * x, axis=-1, keepdims=True) + RMS_EPS)
    o_ref[...] = y * g_ref[...]


def _rms_norm(x, g, *, tm):
    m, d = x.shape
    return pl.pallas_call(
        _rms_norm_kernel,
        out_shape=jax.ShapeDtypeStruct((m, d), F32),
        grid=(m // tm,),
        in_specs=[pl.BlockSpec((tm, d), lambda i: (i, 0)),
                  pl.BlockSpec((1, d), lambda i: (0, 0))],
        out_specs=pl.BlockSpec((tm, d), lambda i: (i, 0)),
        compiler_params=_params("parallel"),
        name="final_norm",
    )(x, g.reshape(1, d))


def _prompt_bias_table(n_heads):
    slopes = _alibi_slopes(n_heads)
    qi = np.arange(BAND)[:, None] + BAND
    kj = np.arange(2 * BAND)[None, :]
    dist = qi - kj
    valid = (dist >= 0) & (dist <= BAND)
    valid_first = valid & (kj >= BAND)
    table = np.empty((n_heads, len(DILATED_PATTERNS), 2, BAND, 2 * BAND), np.float32)
    for h in range(n_heads):
        for p, (_, d) in enumerate(DILATED_PATTERNS):
            bias = -slopes[h] * (dist * d).astype(np.float32)
            table[h, p, 0] = np.where(valid, bias, -np.inf)
            table[h, p, 1] = np.where(valid_first, bias, -np.inf)
    table = table.reshape(n_heads // HEADS_PER_TILE, HEADS_PER_TILE, len(DILATED_PATTERNS), 2, BAND, 2 * BAND)
    return jnp.asarray(table.transpose(0, 2, 3, 1, 4, 5))


def _attn_prompt_kernel(q_ref, k_ref, v_ref, bias_ref, a_ref, o_scr, l_scr, *, seq):
    lane = lax.broadcasted_iota(jnp.int32, (BAND, LANES), 1)
    head0 = lane < HEAD_DIM

    for pi, (_, d) in enumerate(DILATED_PATTERNS):
        nblk = seq // d // BAND

        def rows(start, d=d):
            return pl.ds(start, BAND, stride=d) if d > 1 else pl.ds(start, BAND)

        def block(idx, carry, pi=pi, d=d, nblk=nblk, rows=rows):
            r = idx // nblk
            blk = idx % nblk
            row0 = r + d * BAND * blk
            prev0 = jnp.maximum(row0 - d * BAND, r)
            first = (blk == 0).astype(jnp.int32)
            q = q_ref[rows(row0), :] * (HEAD_DIM ** -0.5)
            kcat = jnp.concatenate([k_ref[rows(prev0), :], k_ref[rows(row0), :]], axis=0).astype(BF16)
            vcat = jnp.concatenate([v_ref[rows(prev0), :], v_ref[rows(row0), :]], axis=0).astype(BF16)
            outs, lses = [], []
            for h in range(HEADS_PER_TILE):
                keep = head0 if h == 0 else jnp.logical_not(head0)
                qh = jnp.where(keep, q, 0.0).astype(BF16)
                s = lax.dot_general(qh, kcat, (((1,), (1,)), ((), ())), preferred_element_type=F32)
                s = s + bias_ref[pi, first, h]
                m = jnp.max(s, axis=-1, keepdims=True)
                p = jnp.exp(s - m)
                den = jnp.sum(p, axis=-1, keepdims=True)
                o = jnp.dot(p.astype(BF16), vcat, preferred_element_type=F32) / den
                outs.append(o)
                lses.append(jnp.broadcast_to(m + jnp.log(den), (BAND, LANES)))
            o_scr[pi, rows(row0), :] = jnp.where(head0, outs[0], outs[1])
            l_scr[pi, rows(row0), :] = jnp.where(head0, lses[0], lses[1])
            return carry

        lax.fori_loop(0, d * nblk, block, 0)

    chunk = 2 * BAND

    def combine(c, carry):
        sl = pl.ds(pl.multiple_of(c * chunk, chunk), chunk)
        ls = [l_scr[pi, sl, :] for pi in range(len(DILATED_PATTERNS))]
        m = jnp.maximum(jnp.maximum(ls[0], ls[1]), ls[2])
        es = [jnp.exp(l - m) for l in ls]
        num = es[0] * o_scr[0, sl, :] + es[1] * o_scr[1, sl, :] + es[2] * o_scr[2, sl, :]
        a_ref[sl, :] = num / (es[0] + es[1] + es[2])
        return carry

    lax.fori_loop(0, seq // chunk, combine, 0)


def _attn_prompt(z, bias, *, batch, seq, q_off, k_off, v_off, d_attn):
    n_pairs = d_attn // LANES
    n_pat = len(DILATED_PATTERNS)

    def col(off):
        return pl.BlockSpec((seq, LANES), lambda b, hp: (b, off // LANES + hp))

    return pl.pallas_call(
        functools.partial(_attn_prompt_kernel, seq=seq),
        out_shape=jax.ShapeDtypeStruct((batch * seq, d_attn), F32),
        grid=(batch, n_pairs),
        in_specs=[
            col(q_off), col(k_off), col(v_off),
            pl.BlockSpec((None, n_pat, 2, HEADS_PER_TILE, BAND, 2 * BAND),
                         lambda b, hp: (hp, 0, 0, 0, 0, 0)),
        ],
        out_specs=pl.BlockSpec((seq, LANES), lambda b, hp: (b, hp)),
        scratch_shapes=[pltpu.VMEM((n_pat, seq, LANES), F32),
                        pltpu.VMEM((n_pat, seq, LANES), F32)],
        compiler_params=_params("parallel", "parallel"),
        name="attn_prompt",
    )(z, z, z, bias)


def _pool_branch(ext_ref, u, pool_w_ref, scale, pos, *, halo, tm, group):
    parts = []
    for g, w in enumerate(POOL_WINDOWS):
        cols = slice(g * group, (g + 1) * group)
        acc = ext_ref[pl.ds(halo, tm), cols]
        for i in range(1, w):
            acc = acc + ext_ref[pl.ds(halo - i, tm), cols]
        cnt = jnp.minimum(w, pos + 1).astype(F32)
        pooled = acc / cnt - u[:, cols]
        parts.append(jnp.dot(pooled.astype(BF16), pool_w_ref[g], preferred_element_type=F32))
    return jnp.concatenate(parts, axis=-1) * scale


def _out_proj_kernel(x_ref, u_ref, halo_ref, gp_ref, ga_ref, a_ref, pw_ref, ps_ref, wo_ref,
                     o_ref, ext_ref, mix_ref, *, tm, tiles_per_seq, d_pool):
    halo = halo_ref.shape[0]
    tile = pl.program_id(0) % tiles_per_seq
    u = u_ref[...]
    ext_ref[pl.ds(0, halo), :] = jnp.where(tile == 0, 0.0, halo_ref[...])
    ext_ref[pl.ds(halo, tm), :] = u
    pos = tile * tm + lax.broadcasted_iota(jnp.int32, (tm, 1), 0)
    pm = _pool_branch(ext_ref, u, pw_ref, ps_ref[...], pos, halo=halo, tm=tm,
                      group=d_pool // len(POOL_WINDOWS))
    mix_ref[:, :d_pool] = (pm * _silu(gp_ref[...])).astype(BF16)
    mix_ref[:, d_pool:] = (a_ref[...] * _silu(ga_ref[...])).astype(BF16)
    o_ref[...] = x_ref[...] + jnp.dot(mix_ref[...], wo_ref[...], preferred_element_type=F32)


def _out_proj(x, z, a, pool_w_bf16, pool_scale, w_out_bf16, *, seq, tm, d_pool, ga_off):
    m, d = x.shape
    halo = 16
    n_grp = len(POOL_WINDOWS)
    grp = d_pool // n_grp
    d_attn = a.shape[1]
    assert ga_off % d_attn == 0 and d_pool == d_attn
    return pl.pallas_call(
        functools.partial(_out_proj_kernel, tm=tm, tiles_per_seq=seq // tm, d_pool=d_pool),
        out_shape=jax.ShapeDtypeStruct((m, d), F32),
        grid=(m // tm,),
        in_specs=[
            pl.BlockSpec((tm, d), lambda i: (i, 0)),
            pl.BlockSpec((tm, d_pool), lambda i: (i, 0)),
            pl.BlockSpec((halo, d_pool), lambda i: (jnp.maximum(i * (tm // halo) - 1, 0), 0)),
            pl.BlockSpec((tm, d_pool), lambda i: (i, 1)),
            pl.BlockSpec((tm, d_attn), lambda i: (i, ga_off // d_attn)),
            pl.BlockSpec((tm, d_attn), lambda i: (i, 0)),
            pl.BlockSpec((n_grp, grp, grp), lambda i: (0, 0, 0)),
            pl.BlockSpec((1, d_pool), lambda i: (0, 0)),
            pl.BlockSpec((d_pool + d_attn, d), lambda i: (0, 0)),
        ],
        out_specs=pl.BlockSpec((tm, d), lambda i: (i, 0)),
        scratch_shapes=[pltpu.VMEM((halo + tm, d_pool), F32),
                        pltpu.VMEM((tm, d_pool + d_attn), BF16)],
        compiler_params=_params("parallel"),
        name="out_proj",
    )(x, z, z, z, z, a, pool_w_bf16, pool_scale.reshape(1, d_pool), w_out_bf16)


def _sample_bias_table(n_heads):
    slopes = _alibi_slopes(n_heads)
    steps = (BAND - np.arange(BAND)).astype(np.float32)
    table = np.stack([-slopes[:, None] * (steps * d)[None, :] for _, d in DILATED_PATTERNS])
    return jnp.asarray(table.astype(np.float32))


def _sample_mix_kernel(z_ref, st_ref, k1_ref, k4_ref, k16_ref, v1_ref, v4_ref, v16_ref,
                       bias_ref, pw_ref, ps_ref, mix_ref, st_out_ref, ext_ref,
                       *, d_pool, d_attn, pos0):
    n_heads = d_attn // HEAD_DIM
    u = z_ref[:, 0:d_pool]
    gp = z_ref[:, d_pool:2 * d_pool]
    q = z_ref[:, 2 * d_pool:2 * d_pool + d_attn]
    k_new = z_ref[:, 2 * d_pool + d_attn:2 * d_pool + 2 * d_attn]
    v_new = z_ref[:, 2 * d_pool + 2 * d_attn:2 * d_pool + 3 * d_attn]
    ga = z_ref[:, 2 * d_pool + 3 * d_attn:2 * d_pool + 4 * d_attn]

    n_state = POOL_MAX - 1
    ext_ref[pl.ds(0, 1), :] = jnp.zeros((1, d_pool), F32)
    ext_ref[pl.ds(1, n_state), :] = st_ref[...]
    ext_ref[pl.ds(POOL_MAX, 1), :] = u
    pos = jnp.full((1, 1), pos0, jnp.int32)
    pm = _pool_branch(ext_ref, u, pw_ref, ps_ref[...], pos, halo=POOL_MAX, tm=1,
                      group=d_pool // len(POOL_WINDOWS))
    st_out_ref[pl.ds(0, n_state - 1), :] = st_ref[pl.ds(1, n_state - 1), :]
    st_out_ref[pl.ds(n_state - 1, 1), :] = u

    head_of_lane = lax.broadcasted_iota(jnp.int32, (n_heads, d_attn), 1) // HEAD_DIM
    diag = head_of_lane == lax.broadcasted_iota(jnp.int32, (n_heads, d_attn), 0)
    q_bd = jnp.where(diag, (q * (HEAD_DIM ** -0.5)).astype(BF16).astype(F32), 0.0)
    k_new_r = k_new.astype(BF16).astype(F32)
    v_new_r = v_new.astype(BF16).astype(F32)
    s_new = jnp.sum(q_bd * k_new_r, axis=-1, keepdims=True)
    outs, lses = [], []
    for pi, (k_ref, v_ref) in enumerate(((k1_ref, v1_ref), (k4_ref, v4_ref), (k16_ref, v16_ref))):
        s = lax.dot_general(q_bd.astype(BF16), k_ref[...].astype(BF16), (((1,), (1,)), ((), ())),
                            preferred_element_type=F32) + bias_ref[pi]
        m = jnp.maximum(jnp.max(s, axis=-1, keepdims=True), s_new)
        p = jnp.exp(s - m)
        p_new = jnp.exp(s_new - m)
        den = jnp.sum(p, axis=-1, keepdims=True) + p_new
        o = jnp.dot(p.astype(BF16), v_ref[...].astype(BF16), preferred_element_type=F32)
        o = (o + p_new.astype(BF16).astype(F32) * v_new_r) / den
        outs.append(o)
        lses.append(m + jnp.log(den))
    m = jnp.maximum(jnp.maximum(lses[0], lses[1]), lses[2])
    es = [jnp.exp(l - m) for l in lses]
    a_bd = (es[0] * outs[0] + es[1] * outs[1] + es[2] * outs[2]) / (es[0] + es[1] + es[2])
    a = jnp.sum(jnp.where(diag, a_bd, 0.0), axis=0, keepdims=True)

    mix_ref[:, :d_pool] = pm * _silu(gp)
    mix_ref[:, d_pool:] = a * _silu(ga)


def _sample_mix(z, state, k_cache, v_cache, layer, bias, pool_w_bf16, pool_scale, *, d_pool, d_attn, pos0):
    bd, n_state, _ = state.shape
    l_win = k_cache.shape[2]
    n_in = z.shape[1]
    n_grp = len(POOL_WINDOWS)
    grp = d_pool // n_grp
    n_heads = d_attn // HEAD_DIM
    n_pat = len(DILATED_PATTERNS)

    def cache_views(c):
        depth = c.shape[0]
        views, specs = [], []
        for _, d in DILATED_PATTERNS:
            n_sub = l_win // d
            views.append(c.reshape(depth, bd, n_sub, d * d_attn))
            specs.append(pl.BlockSpec((None, None, BAND, d_attn),
                                      lambda b, n_sub=n_sub: (layer, b, n_sub // BAND - 1, 0)))
        return views, specs

    k_views, k_specs = cache_views(k_cache)
    v_views, v_specs = cache_views(v_cache)
    return pl.pallas_call(
        functools.partial(_sample_mix_kernel, d_pool=d_pool, d_attn=d_attn, pos0=pos0),
        out_shape=(jax.ShapeDtypeStruct((bd, 1, d_pool + d_attn), F32),
                   jax.ShapeDtypeStruct((bd, n_state, d_pool), F32)),
        grid=(bd,),
        in_specs=[
            pl.BlockSpec((None, 1, n_in), lambda b: (b, 0, 0)),
            pl.BlockSpec((None, n_state, d_pool), lambda b: (b, 0, 0)),
            *k_specs, *v_specs,
            pl.BlockSpec((n_pat, n_heads, BAND), lambda b: (0, 0, 0)),
            pl.BlockSpec((n_grp, grp, grp), lambda b: (0, 0, 0)),
            pl.BlockSpec((1, d_pool), lambda b: (0, 0)),
        ],
        out_specs=(pl.BlockSpec((None, 1, d_pool + d_attn), lambda b: (b, 0, 0)),
                   pl.BlockSpec((None, n_state, d_pool), lambda b: (b, 0, 0))),
        scratch_shapes=[pltpu.VMEM((POOL_MAX + 8, d_pool), F32)],
        compiler_params=_params("parallel"),
        name="sample_mix",
    )(z.reshape(bd, 1, n_in), state, *k_views, *v_views, bias, pool_w_bf16,
      pool_scale.reshape(1, d_pool))


def _resid_matmul_kernel(x_ref, mix_ref, w_ref, o_ref):
    o_ref[...] = x_ref[...] + jnp.dot(mix_ref[...].astype(BF16), w_ref[...], preferred_element_type=F32)


def _resid_matmul(x, mix, w_bf16, *, tn):
    m, d = x.shape
    kdim = mix.shape[1]
    return pl.pallas_call(
        _resid_matmul_kernel,
        out_shape=jax.ShapeDtypeStruct((m, d), F32),
        grid=(d // tn,),
        in_specs=[pl.BlockSpec((m, tn), lambda j: (0, j)),
                  pl.BlockSpec((m, kdim), lambda j: (0, 0)),
                  pl.BlockSpec((kdim, tn), lambda j: (0, j))],
        out_specs=pl.BlockSpec((m, tn), lambda j: (0, j)),
        compiler_params=_params("parallel"),
        name="sample_out_proj",
    )(x, mix, w_bf16)


def kernel(x_prompt, x_sample, state_pool, cache_k_win, cache_v_win, norm_g, w_in, pool_w, pool_scale,
           w_out, final_norm_g):
    batch, seq, d_model = x_prompt.shape
    bd, dec_seq, _ = x_sample.shape
    depth = norm_g.shape[0]
    d_pool = pool_scale.shape[1]
    d_attn = w_out.shape[1] - d_pool
    n_heads = d_attn // HEAD_DIM
    l_win = cache_k_win.shape[2]
    past_len = 16384
    assert dec_seq == 1 and l_win == WIN_MAX and past_len + 1 >= POOL_MAX
    assert seq % (BAND * max(d for _, d in DILATED_PATTERNS)) == 0 and seq >= WIN_MAX
    q_off, k_off, v_off, ga_off = (2 * d_pool, 2 * d_pool + d_attn, 2 * d_pool + 2 * d_attn,
                                   2 * d_pool + 3 * d_attn)

    w_in_b = w_in.astype(BF16)
    w_out_b = w_out.astype(BF16)
    pool_w_b = pool_w.astype(BF16)
    bias_p = _prompt_bias_table(n_heads)
    bias_s = _sample_bias_table(n_heads)

    keep = min(WIN_MAX, seq)
    xp = x_prompt.reshape(batch * seq, d_model)
    pool_p, k_p, v_p = [], [], []
    for l in range(depth):
        z = _in_proj(xp, norm_g[l], w_in_b[l], tm=1024, tn=1024)
        a = _attn_prompt(z, bias_p, batch=batch, seq=seq, q_off=q_off, k_off=k_off, v_off=v_off, d_attn=d_attn)
        xp = _out_proj(xp, z, a, pool_w_b[l], pool_scale[l], w_out_b[l], seq=seq, tm=256, d_pool=d_pool,
                       ga_off=ga_off)
        z3 = z.reshape(batch, seq, -1)
        pool_p.append(z3[:, seq - (POOL_MAX - 1):, :d_pool])
        k_p.append(z3[:, seq - keep:, k_off:k_off + d_attn].reshape(batch, keep, n_heads, HEAD_DIM))
        v_p.append(z3[:, seq - keep:, v_off:v_off + d_attn].reshape(batch, keep, n_heads, HEAD_DIM))
    y_prompt = _rms_norm(xp, final_norm_g, tm=512).reshape(batch, seq, d_model)

    k_cache = cache_k_win.reshape(depth, bd, l_win, d_attn)
    v_cache = cache_v_win.reshape(depth, bd, l_win, d_attn)
    xs = x_sample.reshape(bd, d_model)
    pool_s, k_s, v_s = [], [], []
    for l in range(depth):
        z = _in_proj(xs, norm_g[l], w_in_b[l], tm=bd, tn=1024)
        mix, st = _sample_mix(z, state_pool[l], k_cache, v_cache, l, bias_s, pool_w_b[l], pool_scale[l],
                              d_pool=d_pool, d_attn=d_attn, pos0=past_len)
        xs = _resid_matmul(xs, mix.reshape(bd, -1), w_out_b[l], tn=1024)
        pool_s.append(st)
        k_s.append(z[:, k_off:k_off + d_attn].reshape(bd, 1, n_heads, HEAD_DIM))
        v_s.append(z[:, v_off:v_off + d_attn].reshape(bd, 1, n_heads, HEAD_DIM))
    y_sample = _rms_norm(xs, final_norm_g, tm=bd).reshape(bd, 1, d_model)

    return (y_prompt, y_sample,
            jnp.stack(pool_p), jnp.stack(k_p), jnp.stack(v_p),
            jnp.stack(pool_s), jnp.stack(k_s), jnp.stack(v_s))
```

```python
import functools

import numpy as np
import jax
import jax.numpy as jnp
from jax import lax
from jax.experimental import pallas as pl
from jax.experimental.pallas import tpu as pltpu

HEAD_DIM = 64
LANES = 128
SUBLANES = 8
HEADS_PER_TILE = LANES // HEAD_DIM
POOL_WINDOWS = (2, 4, 8, 16)
POOL_MAX = max(POOL_WINDOWS)
DILATED_PATTERNS = ((128, 1), (512, 4), (2048, 16))
N_PATTERNS = len(DILATED_PATTERNS)
BAND = 128
WIN_MAX = max(w for w, _ in DILATED_PATTERNS)
PAST_LEN = 16384
RMS_EPS = 1e-6
VMEM_LIMIT_BYTES = 56 * 2**20

F32 = jnp.float32
BF16 = jnp.bfloat16
NT_DIMS = (((1,), (1,)), ((), ()))


def _alibi_slopes(n_heads):
    return (2.0 ** (-8.0 * np.arange(1, n_heads + 1) / n_heads)).astype(np.float32)


def _silu(x):
    return x * (1.0 / (1.0 + jnp.exp(-x)))


def _params(*sem):
    return pltpu.CompilerParams(dimension_semantics=sem, vmem_limit_bytes=VMEM_LIMIT_BYTES)


def _normed_bf16(x_ref, g_ref, h_ref):
    x = x_ref[...]
    y = x * lax.rsqrt(jnp.mean(x * x, axis=-1, keepdims=True) + RMS_EPS)
    h_ref[...] = (y * g_ref[...]).astype(BF16)


def _in_proj_kernel(x_ref, g_ref, w_ref, z_ref, h_ref):
    @pl.when(pl.program_id(1) == 0)
    def _():
        _normed_bf16(x_ref, g_ref, h_ref)

    z_ref[...] = jnp.dot(h_ref[...], w_ref[...], preferred_element_type=F32)


def _in_proj_window_kernel(x_ref, g_ref, w_ref, z_ref, kt_ref, vt_ref, h_ref,
                           *, tiles_per_seq, first_win_tile, k_tile, v_tile):
    j = pl.program_id(1)

    @pl.when(j == 0)
    def _():
        _normed_bf16(x_ref, g_ref, h_ref)

    z_ref[...] = jnp.dot(h_ref[...], w_ref[...], preferred_element_type=F32)
    in_window = pl.program_id(0) % tiles_per_seq >= first_win_tile

    @pl.when(jnp.logical_and(in_window, j == k_tile))
    def _():
        kt_ref[...] = z_ref[...].T

    @pl.when(jnp.logical_and(in_window, j == v_tile))
    def _():
        vt_ref[...] = z_ref[...].T


def _in_proj(x, g, w_bf16, layer, *, tm, tn):
    m, d = x.shape
    n = w_bf16.shape[2]
    return pl.pallas_call(
        _in_proj_kernel,
        out_shape=jax.ShapeDtypeStruct((m, n), F32),
        grid=(m // tm, n // tn),
        in_specs=[
            pl.BlockSpec((tm, d), lambda i, j: (i, 0)),
            pl.BlockSpec((1, d), lambda i, j: (0, 0)),
            pl.BlockSpec((None, d, tn), lambda i, j: (layer, 0, j)),
        ],
        out_specs=pl.BlockSpec((tm, tn), lambda i, j: (i, j)),
        scratch_shapes=[pltpu.VMEM((tm, d), BF16)],
        compiler_params=_params("parallel", "arbitrary"),
        name="in_proj",
    )(x, g.reshape(1, d), w_bf16)


def _in_proj_window(x, g, w_bf16, layer, *, seq, keep, k_off, v_off, d_attn, tm):
    m, d = x.shape
    n = w_bf16.shape[2]
    tn = d_attn
    assert k_off % tn == 0 and v_off % tn == 0 and (seq - keep) % tm == 0 and seq % tm == 0
    tiles_per_seq = seq // tm
    first_win_tile = (seq - keep) // tm

    def win_map(i, j):
        return (i // tiles_per_seq, 0, jnp.maximum(i % tiles_per_seq - first_win_tile, 0))

    win = jax.ShapeDtypeStruct((m // seq, d_attn, keep), F32)
    return pl.pallas_call(
        functools.partial(_in_proj_window_kernel, tiles_per_seq=tiles_per_seq,
                          first_win_tile=first_win_tile, k_tile=k_off // tn, v_tile=v_off // tn),
        out_shape=(jax.ShapeDtypeStruct((m, n), F32), win, win),
        grid=(m // tm, n // tn),
        in_specs=[
            pl.BlockSpec((tm, d), lambda i, j: (i, 0)),
            pl.BlockSpec((1, d), lambda i, j: (0, 0)),
            pl.BlockSpec((None, d, tn), lambda i, j: (layer, 0, j)),
        ],
        out_specs=(pl.BlockSpec((tm, tn), lambda i, j: (i, j)),
                   pl.BlockSpec((None, d_attn, tm), win_map),
                   pl.BlockSpec((None, d_attn, tm), win_map)),
        scratch_shapes=[pltpu.VMEM((tm, d), BF16)],
        compiler_params=_params("arbitrary", "arbitrary"),
        name="in_proj_prompt",
    )(x, g.reshape(1, d), w_bf16)


def _rms_norm_kernel(x_ref, g_ref, o_ref):
    x = x_ref[...]
    y = x * lax.rsqrt(jnp.mean(x * x, axis=-1, keepdims=True) + RMS_EPS)
    o_ref[...] = y * g_ref[...]


def _rms_norm(x, g, *, tm):
    m, d = x.shape
    return pl.pallas_call(
        _rms_norm_kernel,
        out_shape=jax.ShapeDtypeStruct((m, d), F32),
        grid=(m // tm,),
        in_specs=[pl.BlockSpec((tm, d), lambda i: (i, 0)),
                  pl.BlockSpec((1, d), lambda i: (0, 0))],
        out_specs=pl.BlockSpec((tm, d), lambda i: (i, 0)),
        compiler_params=_params("parallel"),
        name="final_norm",
    )(x, g.reshape(1, d))


def _prompt_bias_table(n_heads):
    slopes = _alibi_slopes(n_heads)
    qi = np.arange(BAND)[:, None] + BAND
    kj = np.arange(2 * BAND)[None, :]
    dist = qi - kj
    valid = (dist >= 0) & (dist <= BAND)
    valid_first = valid & (kj >= BAND)
    table = np.empty((n_heads, N_PATTERNS, 2, BAND, 2 * BAND), np.float32)
    for h in range(n_heads):
        for p, (_, d) in enumerate(DILATED_PATTERNS):
            bias = -slopes[h] * (dist * d).astype(np.float32)
            table[h, p, 0] = np.where(valid, bias, -np.inf)
            table[h, p, 1] = np.where(valid_first, bias, -np.inf)
    table = table.reshape(n_heads // HEADS_PER_TILE, HEADS_PER_TILE, N_PATTERNS, 2, BAND, 2 * BAND)
    table = table.transpose(0, 2, 3, 1, 4, 5)
    return jnp.asarray(table.reshape(n_heads // HEADS_PER_TILE, N_PATTERNS, 2, HEADS_PER_TILE * BAND, 2 * BAND))


ATTN_GROUP = 4
SOFTMAX_ROWS = 32


def _attn_prompt_kernel(q_ref, k_ref, v_ref, bias_ref, a_ref,
                        o_scr, l_scr, s_scr, p_scr, v_scr, inv_scr, lse_scr, *, seq):
    lane = lax.broadcasted_iota(jnp.int32, (BAND, LANES), 1)
    head0 = lane < HEAD_DIM
    stacked = HEADS_PER_TILE * BAND

    for pi, (_, d) in enumerate(DILATED_PATTERNS):
        nblk = seq // d // BAND

        def rows(start, d=d):
            return pl.ds(start, BAND, stride=d) if d > 1 else pl.ds(start, BAND)

        def group(it, carry, pi=pi, d=d, nblk=nblk, rows=rows):
            units = []
            for g in range(ATTN_GROUP):
                idx = it * ATTN_GROUP + g
                r = idx // nblk
                blk = idx % nblk
                row0 = r + d * BAND * blk
                prev0 = jnp.maximum(row0 - d * BAND, r)
                units.append((row0, prev0, (blk == 0).astype(jnp.int32)))

            for g, (row0, prev0, _) in enumerate(units):
                q = q_ref[rows(row0), :] * (HEAD_DIM ** -0.5)
                q2 = jnp.concatenate([jnp.where(head0, q, 0.0), jnp.where(head0, 0.0, q)], axis=0).astype(BF16)
                kcat = jnp.concatenate([k_ref[rows(prev0), :], k_ref[rows(row0), :]], axis=0).astype(BF16)
                s_scr[g] = lax.dot_general(q2, kcat, NT_DIMS, preferred_element_type=F32)
                v_scr[g] = jnp.concatenate([v_ref[rows(prev0), :], v_ref[rows(row0), :]], axis=0).astype(BF16)

            for g, (_, _, first) in enumerate(units):
                for c in range(stacked // SOFTMAX_ROWS):
                    sl = pl.ds(c * SOFTMAX_ROWS, SOFTMAX_ROWS)
                    s = s_scr[g, sl, :] + bias_ref[pi, first, sl, :]
                    m = jnp.max(s, axis=-1, keepdims=True)
                    p = jnp.exp(s - m)
                    den = jnp.sum(p, axis=-1, keepdims=True)
                    p_scr[g, sl, :] = p.astype(BF16)
                    inv_scr[g, sl, :] = jnp.broadcast_to(1.0 / den, (SOFTMAX_ROWS, LANES))
                    lse_scr[g, sl, :] = jnp.broadcast_to(m + jnp.log(den), (SOFTMAX_ROWS, LANES))

            for g, (row0, _, _) in enumerate(units):
                o2 = jnp.dot(p_scr[g], v_scr[g], preferred_element_type=F32) * inv_scr[g]
                o_scr[pi, rows(row0), :] = jnp.where(head0, o2[:BAND], o2[BAND:])
                l_scr[pi, rows(row0), :] = jnp.where(head0, lse_scr[g, pl.ds(0, BAND), :],
                                                     lse_scr[g, pl.ds(BAND, BAND), :])
            return carry

        lax.fori_loop(0, d * nblk // ATTN_GROUP, group, 0)

    chunk = 2 * BAND

    def combine(c, carry):
        sl = pl.ds(pl.multiple_of(c * chunk, chunk), chunk)
        ls = [l_scr[pi, sl, :] for pi in range(N_PATTERNS)]
        m = jnp.maximum(jnp.maximum(ls[0], ls[1]), ls[2])
        es = [jnp.exp(l - m) for l in ls]
        num = es[0] * o_scr[0, sl, :] + es[1] * o_scr[1, sl, :] + es[2] * o_scr[2, sl, :]
        a_ref[sl, :] = num / (es[0] + es[1] + es[2])
        return carry

    lax.fori_loop(0, seq // chunk, combine, 0)


def _attn_prompt(z, bias, *, batch, seq, q_off, k_off, v_off, d_attn):
    n_pairs = d_attn // LANES
    stacked = HEADS_PER_TILE * BAND

    def col(off):
        return pl.BlockSpec((seq, LANES), lambda b, hp: (b, off // LANES + hp))

    return pl.pallas_call(
        functools.partial(_attn_prompt_kernel, seq=seq),
        out_shape=jax.ShapeDtypeStruct((batch * seq, d_attn), F32),
        grid=(batch, n_pairs),
        in_specs=[
            col(q_off), col(k_off), col(v_off),
            pl.BlockSpec((None, N_PATTERNS, 2, stacked, 2 * BAND), lambda b, hp: (hp, 0, 0, 0, 0)),
        ],
        out_specs=pl.BlockSpec((seq, LANES), lambda b, hp: (b, hp)),
        scratch_shapes=[pltpu.VMEM((N_PATTERNS, seq, LANES), F32),
                        pltpu.VMEM((N_PATTERNS, seq, LANES), F32),
                        pltpu.VMEM((ATTN_GROUP, stacked, 2 * BAND), F32),
                        pltpu.VMEM((ATTN_GROUP, stacked, 2 * BAND), BF16),
                        pltpu.VMEM((ATTN_GROUP, 2 * BAND, LANES), BF16),
                        pltpu.VMEM((ATTN_GROUP, stacked, LANES), F32),
                        pltpu.VMEM((ATTN_GROUP, stacked, LANES), F32)],
        compiler_params=_params("parallel", "parallel"),
        name="attn_prompt",
    )(z, z, z, bias)


def _pool_branch(ext_ref, u, pool_w_ref, scale, pos, *, halo, tm, group):
    parts = []
    for g, w in enumerate(POOL_WINDOWS):
        cols = slice(g * group, (g + 1) * group)
        acc = ext_ref[pl.ds(halo, tm), cols]
        for i in range(1, w):
            acc = acc + ext_ref[pl.ds(halo - i, tm), cols]
        cnt = jnp.minimum(w, pos + 1).astype(F32)
        pooled = acc / cnt - u[:, cols]
        parts.append(jnp.dot(pooled.astype(BF16), pool_w_ref[g], preferred_element_type=F32))
    return jnp.concatenate(parts, axis=-1) * scale


def _out_proj_kernel(x_ref, u_ref, halo_ref, gp_ref, ga_ref, a_ref, pw_ref, ps_ref, wo_ref,
                     o_ref, ext_ref, mix_ref, *, tm, tiles_per_seq, d_pool):
    halo = halo_ref.shape[0]
    tile = pl.program_id(0) % tiles_per_seq
    u = u_ref[...]
    ext_ref[pl.ds(0, halo), :] = jnp.where(tile == 0, 0.0, halo_ref[...])
    ext_ref[pl.ds(halo, tm), :] = u
    pos = tile * tm + lax.broadcasted_iota(jnp.int32, (tm, 1), 0)
    pm = _pool_branch(ext_ref, u, pw_ref, ps_ref[...], pos, halo=halo, tm=tm,
                      group=d_pool // len(POOL_WINDOWS))
    mix_ref[:, :d_pool] = (pm * _silu(gp_ref[...])).astype(BF16)
    mix_ref[:, d_pool:] = (a_ref[...] * _silu(ga_ref[...])).astype(BF16)
    o_ref[...] = x_ref[...] + jnp.dot(mix_ref[...], wo_ref[...], preferred_element_type=F32)


def _out_proj(x, z, a, pool_w_bf16, pool_scale, w_out_bf16, layer, *, seq, tm, d_pool, ga_off):
    m, d = x.shape
    halo = 2 * SUBLANES
    n_grp = len(POOL_WINDOWS)
    grp = d_pool // n_grp
    d_attn = a.shape[1]
    assert ga_off % d_attn == 0 and d_pool == d_attn and halo >= POOL_MAX - 1
    return pl.pallas_call(
        functools.partial(_out_proj_kernel, tm=tm, tiles_per_seq=seq // tm, d_pool=d_pool),
        out_shape=jax.ShapeDtypeStruct((m, d), F32),
        grid=(m // tm,),
        in_specs=[
            pl.BlockSpec((tm, d), lambda i: (i, 0)),
            pl.BlockSpec((tm, d_pool), lambda i: (i, 0)),
            pl.BlockSpec((halo, d_pool), lambda i: (jnp.maximum(i * (tm // halo) - 1, 0), 0)),
            pl.BlockSpec((tm, d_pool), lambda i: (i, 1)),
            pl.BlockSpec((tm, d_attn), lambda i: (i, ga_off // d_attn)),
            pl.BlockSpec((tm, d_attn), lambda i: (i, 0)),
            pl.BlockSpec((None, n_grp, grp, grp), lambda i: (layer, 0, 0, 0)),
            pl.BlockSpec((1, d_pool), lambda i: (0, 0)),
            pl.BlockSpec((None, d_pool + d_attn, d), lambda i: (layer, 0, 0)),
        ],
        out_specs=pl.BlockSpec((tm, d), lambda i: (i, 0)),
        scratch_shapes=[pltpu.VMEM((halo + tm, d_pool), F32),
                        pltpu.VMEM((tm, d_pool + d_attn), BF16)],
        compiler_params=_params("parallel"),
        name="out_proj",
    )(x, z, z, z, z, a, pool_w_bf16, pool_scale.reshape(1, d_pool), w_out_bf16)


def _sample_bias_table(n_heads, l_win):
    slopes = _alibi_slopes(n_heads)
    dist = l_win - np.arange(l_win)
    table = np.empty((n_heads, SUBLANES, l_win), np.float32)
    for p, (window, d) in enumerate(DILATED_PATTERNS):
        valid = (dist % d == 0) & (dist <= window)
        table[:, p, :] = np.where(valid[None, :], -slopes[:, None] * dist[None, :].astype(np.float32), -np.inf)
    table[:, N_PATTERNS:, :] = table[:, :1, :]
    return jnp.asarray(table)


def _sample_mix_kernel(z_ref, st_ref, kt_ref, vt_ref, bias_ref, pw_ref, ps_ref,
                       mix_ref, st_out_ref, ext_ref, a_scr, *, d_pool, d_attn, pos0):
    n_heads = d_attn // HEAD_DIM
    u = z_ref[:, 0:d_pool]
    gp = z_ref[:, d_pool:2 * d_pool]
    ga = z_ref[:, 2 * d_pool + 3 * d_attn:2 * d_pool + 4 * d_attn]
    q_off, k_off, v_off = 2 * d_pool, 2 * d_pool + d_attn, 2 * d_pool + 2 * d_attn

    n_state = POOL_MAX - 1
    ext_ref[pl.ds(0, 1), :] = jnp.zeros((1, d_pool), F32)
    ext_ref[pl.ds(1, n_state), :] = st_ref[...]
    ext_ref[pl.ds(POOL_MAX, 1), :] = u
    pos = jnp.full((1, 1), pos0, jnp.int32)
    pm = _pool_branch(ext_ref, u, pw_ref, ps_ref[...], pos, halo=POOL_MAX, tm=1,
                      group=d_pool // len(POOL_WINDOWS))
    st_out_ref[pl.ds(0, n_state - 1), :] = st_ref[pl.ds(1, n_state - 1), :]
    st_out_ref[pl.ds(n_state - 1, 1), :] = u

    is_pattern = lax.broadcasted_iota(jnp.int32, (SUBLANES, 1), 0) < N_PATTERNS
    for h in range(n_heads):
        cols = slice(h * HEAD_DIM, (h + 1) * HEAD_DIM)

        def head_row(off, cols=cols):
            return z_ref[:, off:off + d_attn][:, cols]

        q8 = jnp.broadcast_to(head_row(q_off) * (HEAD_DIM ** -0.5), (SUBLANES, HEAD_DIM)).astype(BF16)
        k_new = head_row(k_off).astype(BF16).astype(F32)
        v_new = head_row(v_off).astype(BF16).astype(F32)
        s = jnp.dot(q8, kt_ref[h].astype(BF16), preferred_element_type=F32) + bias_ref[h]
        s_new = jnp.sum(q8.astype(F32) * k_new, axis=-1, keepdims=True)
        m = jnp.maximum(jnp.max(s, axis=-1, keepdims=True), s_new)
        p = jnp.exp(s - m)
        p_new = jnp.exp(s_new - m)
        den = jnp.sum(p, axis=-1, keepdims=True) + p_new
        o = lax.dot_general(p.astype(BF16), vt_ref[h].astype(BF16), NT_DIMS, preferred_element_type=F32)
        o = (o + p_new.astype(BF16).astype(F32) * v_new) / den
        lse = jnp.where(is_pattern, m + jnp.log(den), -jnp.inf)
        e = jnp.exp(lse - jnp.max(lse, axis=0, keepdims=True))
        a_scr[:, cols] = jnp.sum(e * o, axis=0, keepdims=True) / jnp.sum(e, axis=0, keepdims=True)

    mix_ref[:, :d_pool] = pm * _silu(gp)
    mix_ref[:, d_pool:] = a_scr[...] * _silu(ga)


def _sample_mix(z, state, kt_cache, vt_cache, layer, bias, pool_w_bf16, pool_scale, *, d_pool, d_attn, pos0):
    bd, n_state, _ = state.shape
    n_heads, _, l_win = kt_cache.shape[2:]
    n_in = z.shape[1]
    n_grp = len(POOL_WINDOWS)
    grp = d_pool // n_grp
    cache_spec = pl.BlockSpec((None, None, n_heads, HEAD_DIM, l_win), lambda b: (layer, b, 0, 0, 0))
    return pl.pallas_call(
        functools.partial(_sample_mix_kernel, d_pool=d_pool, d_attn=d_attn, pos0=pos0),
        out_shape=(jax.ShapeDtypeStruct((bd, 1, d_pool + d_attn), F32),
                   jax.ShapeDtypeStruct((bd, n_state, d_pool), F32)),
        grid=(bd,),
        in_specs=[
            pl.BlockSpec((None, 1, n_in), lambda b: (b, 0, 0)),
            pl.BlockSpec((None, n_state, d_pool), lambda b: (b, 0, 0)),
            cache_spec, cache_spec,
            pl.BlockSpec((n_heads, SUBLANES, l_win), lambda b: (0, 0, 0)),
            pl.BlockSpec((None, n_grp, grp, grp), lambda b: (layer, 0, 0, 0)),
            pl.BlockSpec((1, d_pool), lambda b: (0, 0)),
        ],
        out_specs=(pl.BlockSpec((None, 1, d_pool + d_attn), lambda b: (b, 0, 0)),
                   pl.BlockSpec((None, n_state, d_pool), lambda b: (b, 0, 0))),
        scratch_shapes=[pltpu.VMEM((POOL_MAX + SUBLANES, d_pool), F32),
                        pltpu.VMEM((1, d_attn), F32)],
        compiler_params=_params("parallel"),
        name="sample_mix",
    )(z.reshape(bd, 1, n_in), state, kt_cache, vt_cache, bias, pool_w_bf16,
      pool_scale.reshape(1, d_pool))


def _resid_matmul_kernel(x_ref, mix_ref, w_ref, o_ref):
    o_ref[...] = x_ref[...] + jnp.dot(mix_ref[...].astype(BF16), w_ref[...], preferred_element_type=F32)


def _resid_matmul(x, mix, w_bf16, layer, *, tn):
    m, d = x.shape
    kdim = mix.shape[1]
    return pl.pallas_call(
        _resid_matmul_kernel,
        out_shape=jax.ShapeDtypeStruct((m, d), F32),
        grid=(d // tn,),
        in_specs=[pl.BlockSpec((m, tn), lambda j: (0, j)),
                  pl.BlockSpec((m, kdim), lambda j: (0, 0)),
                  pl.BlockSpec((None, kdim, tn), lambda j: (layer, 0, j))],
        out_specs=pl.BlockSpec((m, tn), lambda j: (0, j)),
        compiler_params=_params("parallel"),
        name="sample_out_proj",
    )(x, mix, w_bf16)


def kernel(x_prompt, x_sample, state_pool, cache_k_win, cache_v_win, norm_g, w_in, pool_w, pool_scale,
           w_out, final_norm_g):
    batch, seq, d_model = x_prompt.shape
    bd, dec_seq, _ = x_sample.shape
    depth = norm_g.shape[0]
    d_pool = pool_scale.shape[1]
    d_attn = w_out.shape[1] - d_pool
    n_heads = d_attn // HEAD_DIM
    l_win = cache_k_win.shape[2]
    assert dec_seq == 1 and l_win == WIN_MAX and PAST_LEN + 1 >= POOL_MAX
    assert seq % (BAND * max(d for _, d in DILATED_PATTERNS)) == 0 and seq >= WIN_MAX
    q_off, k_off, v_off, ga_off = (2 * d_pool, 2 * d_pool + d_attn, 2 * d_pool + 2 * d_attn,
                                   2 * d_pool + 3 * d_attn)

    w_in_b = w_in.astype(BF16)
    w_out_b = w_out.astype(BF16)
    pool_w_b = pool_w.astype(BF16)
    bias_p = _prompt_bias_table(n_heads)
    bias_s = _sample_bias_table(n_heads, l_win)

    keep = min(WIN_MAX, seq)
    xp = x_prompt.reshape(batch * seq, d_model)
    pool_p, kt_p, vt_p = [], [], []
    for l in range(depth):
        z, kt, vt = _in_proj_window(xp, norm_g[l], w_in_b, l, seq=seq, keep=keep, k_off=k_off, v_off=v_off,
                                    d_attn=d_attn, tm=1024)
        a = _attn_prompt(z, bias_p, batch=batch, seq=seq, q_off=q_off, k_off=k_off, v_off=v_off, d_attn=d_attn)
        xp = _out_proj(xp, z, a, pool_w_b, pool_scale[l], w_out_b, l, seq=seq, tm=256, d_pool=d_pool,
                       ga_off=ga_off)
        pool_p.append(z.reshape(batch, seq, -1)[:, seq - (POOL_MAX - 1):, :d_pool])
        kt_p.append(kt)
        vt_p.append(vt)
    y_prompt = _rms_norm(xp, final_norm_g, tm=512).reshape(batch, seq, d_model)

    def from_slabs(slabs):
        return jnp.stack(slabs).reshape(depth, batch, n_heads, HEAD_DIM, keep).transpose(0, 1, 4, 2, 3)

    kt_cache = cache_k_win.transpose(0, 1, 3, 4, 2)
    vt_cache = cache_v_win.transpose(0, 1, 3, 4, 2)
    xs = x_sample.reshape(bd, d_model)
    pool_s, k_s, v_s = [], [], []
    for l in range(depth):
        z = _in_proj(xs, norm_g[l], w_in_b, l, tm=bd, tn=1024)
        mix, st = _sample_mix(z, state_pool[l], kt_cache, vt_cache, l, bias_s, pool_w_b, pool_scale[l],
                              d_pool=d_pool, d_attn=d_attn, pos0=PAST_LEN)
        xs = _resid_matmul(xs, mix.reshape(bd, -1), w_out_b, l, tn=1024)
        pool_s.append(st)
        k_s.append(z[:, k_off:k_off + d_attn].reshape(bd, 1, n_heads, HEAD_DIM))
        v_s.append(z[:, v_off:v_off + d_attn].reshape(bd, 1, n_heads, HEAD_DIM))
    y_sample = _rms_norm(xs, final_norm_g, tm=bd).reshape(bd, 1, d_model)

    return (y_prompt, y_sample,
            jnp.stack(pool_p), from_slabs(kt_p), from_slabs(vt_p),
            jnp.stack(pool_s), jnp.stack(k_s), jnp.stack(v_s))
```

```python
import functools

import numpy as np
import jax
import jax.numpy as jnp
from jax import lax
from jax.experimental import pallas as pl
from jax.experimental.pallas import tpu as pltpu

HEAD_DIM = 64
LANES = 128
SUBLANES = 8
HEADS_PER_TILE = LANES // HEAD_DIM
POOL_WINDOWS = (2, 4, 8, 16)
POOL_MAX = max(POOL_WINDOWS)
DILATED_PATTERNS = ((128, 1), (512, 4), (2048, 16))
N_PATTERNS = len(DILATED_PATTERNS)
BAND = 128
WIN_MAX = max(w for w, _ in DILATED_PATTERNS)
PAST_LEN = 16384
RMS_EPS = 1e-6
VMEM_LIMIT_BYTES = 56 * 2**20

F32 = jnp.float32
BF16 = jnp.bfloat16
NT_DIMS = (((1,), (1,)), ((), ()))


def _alibi_slopes(n_heads):
    return (2.0 ** (-8.0 * np.arange(1, n_heads + 1) / n_heads)).astype(np.float32)


def _silu(x):
    return x * (1.0 / (1.0 + jnp.exp(-x)))


def _params(*sem):
    return pltpu.CompilerParams(dimension_semantics=sem, vmem_limit_bytes=VMEM_LIMIT_BYTES)


def _normed_bf16(x_ref, g_ref, h_ref):
    x = x_ref[...]
    y = x * lax.rsqrt(jnp.mean(x * x, axis=-1, keepdims=True) + RMS_EPS)
    h_ref[...] = (y * g_ref[...]).astype(BF16)


def _in_proj_kernel(x_ref, g_ref, w_ref, z_ref, h_ref):
    @pl.when(pl.program_id(1) == 0)
    def _():
        _normed_bf16(x_ref, g_ref, h_ref)

    z_ref[...] = jnp.dot(h_ref[...], w_ref[...], preferred_element_type=F32)


def _in_proj_window_kernel(x_ref, g_ref, w_ref, z_ref, kt_ref, vt_ref, h_ref,
                           *, tiles_per_seq, first_win_tile, k_tile, v_tile):
    j = pl.program_id(1)

    @pl.when(j == 0)
    def _():
        _normed_bf16(x_ref, g_ref, h_ref)

    z_ref[...] = jnp.dot(h_ref[...], w_ref[...], preferred_element_type=F32)
    in_window = pl.program_id(0) % tiles_per_seq >= first_win_tile

    @pl.when(jnp.logical_and(in_window, j == k_tile))
    def _():
        kt_ref[...] = z_ref[...].T

    @pl.when(jnp.logical_and(in_window, j == v_tile))
    def _():
        vt_ref[...] = z_ref[...].T


def _in_proj(x, g, w_bf16, layer, *, tm, tn):
    m, d = x.shape
    n = w_bf16.shape[2]
    return pl.pallas_call(
        _in_proj_kernel,
        out_shape=jax.ShapeDtypeStruct((m, n), F32),
        grid=(m // tm, n // tn),
        in_specs=[
            pl.BlockSpec((tm, d), lambda i, j: (i, 0)),
            pl.BlockSpec((1, d), lambda i, j: (0, 0)),
            pl.BlockSpec((None, d, tn), lambda i, j: (layer, 0, j)),
        ],
        out_specs=pl.BlockSpec((tm, tn), lambda i, j: (i, j)),
        scratch_shapes=[pltpu.VMEM((tm, d), BF16)],
        compiler_params=_params("parallel", "arbitrary"),
        name="in_proj",
    )(x, g.reshape(1, d), w_bf16)


def _in_proj_window(x, g, w_bf16, layer, *, seq, keep, k_off, v_off, d_attn, tm):
    m, d = x.shape
    n = w_bf16.shape[2]
    tn = d_attn
    assert k_off % tn == 0 and v_off % tn == 0 and (seq - keep) % tm == 0 and seq % tm == 0
    tiles_per_seq = seq // tm
    first_win_tile = (seq - keep) // tm

    def win_map(i, j):
        return (i // tiles_per_seq, 0, jnp.maximum(i % tiles_per_seq - first_win_tile, 0))

    win = jax.ShapeDtypeStruct((m // seq, d_attn, keep), F32)
    return pl.pallas_call(
        functools.partial(_in_proj_window_kernel, tiles_per_seq=tiles_per_seq,
                          first_win_tile=first_win_tile, k_tile=k_off // tn, v_tile=v_off // tn),
        out_shape=(jax.ShapeDtypeStruct((m, n), F32), win, win),
        grid=(m // tm, n // tn),
        in_specs=[
            pl.BlockSpec((tm, d), lambda i, j: (i, 0)),
            pl.BlockSpec((1, d), lambda i, j: (0, 0)),
            pl.BlockSpec((None, d, tn), lambda i, j: (layer, 0, j)),
        ],
        out_specs=(pl.BlockSpec((tm, tn), lambda i, j: (i, j)),
                   pl.BlockSpec((None, d_attn, tm), win_map),
                   pl.BlockSpec((None, d_attn, tm), win_map)),
        scratch_shapes=[pltpu.VMEM((tm, d), BF16)],
        compiler_params=_params("arbitrary", "arbitrary"),
        name="in_proj_prompt",
    )(x, g.reshape(1, d), w_bf16)


def _rms_norm_kernel(x_ref, g_ref, o_ref):
    x = x_ref[...]
    y = x * lax.rsqrt(jnp.mean(x * x, axis=-1, keepdims=True) + RMS_EPS)
    o_ref[...] = y * g_ref[...]


def _rms_norm(x, g, *, tm):
    m, d = x.shape
    return pl.pallas_call(
        _rms_norm_kernel,
        out_shape=jax.ShapeDtypeStruct((m, d), F32),
        grid=(m // tm,),
        in_specs=[pl.BlockSpec((tm, d), lambda i: (i, 0)),
                  pl.BlockSpec((1, d), lambda i: (0, 0))],
        out_specs=pl.BlockSpec((tm, d), lambda i: (i, 0)),
        compiler_params=_params("parallel"),
        name="final_norm",
    )(x, g.reshape(1, d))


def _prompt_bias_table(n_heads):
    slopes = _alibi_slopes(n_heads)
    qi = np.arange(BAND)[:, None] + BAND
    kj = np.arange(2 * BAND)[None, :]
    dist = qi - kj
    valid = (dist >= 0) & (dist <= BAND)
    valid_first = valid & (kj >= BAND)
    table = np.empty((n_heads, N_PATTERNS, 2, BAND, 2 * BAND), np.float32)
    for h in range(n_heads):
        for p, (_, d) in enumerate(DILATED_PATTERNS):
            bias = -slopes[h] * (dist * d).astype(np.float32)
            table[h, p, 0] = np.where(valid, bias, -np.inf)
            table[h, p, 1] = np.where(valid_first, bias, -np.inf)
    table = table.reshape(n_heads // HEADS_PER_TILE, HEADS_PER_TILE, N_PATTERNS, 2, BAND, 2 * BAND)
    table = table.transpose(0, 2, 3, 1, 4, 5)
    return jnp.asarray(table.reshape(n_heads // HEADS_PER_TILE, N_PATTERNS, 2, HEADS_PER_TILE * BAND, 2 * BAND))


ATTN_GROUP = 8
SOFTMAX_ROWS = 32


PREP_ROWS = 256


def _attn_prompt_kernel(q_ref, k_ref, v_ref, bias_ref, a_ref,
                        o_scr, m_scr, d_scr, qd_scrs, kd_scrs, vd_scrs,
                        s_scr, p_scr, mg_scr, dg_scr, *, seq):
    lane = lax.broadcasted_iota(jnp.int32, (BAND, LANES), 1)
    head0 = lane < HEAD_DIM
    stacked = HEADS_PER_TILE * BAND
    scale = HEAD_DIM ** -0.5
    srcs = (q_ref, k_ref, v_ref)
    assert DILATED_PATTERNS[1][1] ** 2 == DILATED_PATTERNS[2][1] and DILATED_PATTERNS[0][1] == 1
    d_mid, d_big = DILATED_PATTERNS[1][1], DILATED_PATTERNS[2][1]

    def put(ti, pi, sub, row, val):
        n_sub = seq // DILATED_PATTERNS[pi][1]
        if ti == 0:
            qd_scrs[pi][pl.ds(sub * n_sub + row, PREP_ROWS), :] = (val * scale).astype(BF16)
        else:
            dst = (kd_scrs, vd_scrs)[ti - 1][pi]
            dst[pl.ds(sub * (BAND + n_sub) + BAND + row, PREP_ROWS), :] = val.astype(BF16)

    for pi, (_, d) in enumerate(DILATED_PATTERNS):
        n_sub = seq // d
        for dst in (kd_scrs[pi], vd_scrs[pi]):
            for r in range(d):
                dst[pl.ds(r * (BAND + n_sub), BAND), :] = jnp.zeros((BAND, LANES), BF16)

    def prep_dense(c, carry):
        row = pl.multiple_of(c * PREP_ROWS, PREP_ROWS)
        for ti in range(3):
            put(ti, 0, 0, row, srcs[ti][pl.ds(row, PREP_ROWS), :])
        return carry

    lax.fori_loop(0, seq // PREP_ROWS, prep_dense, 0)

    n_mid = seq // d_mid

    def prep_mid(c, carry):
        row = pl.multiple_of(c * PREP_ROWS, PREP_ROWS)
        for ti in range(3):
            for r in range(d_mid):
                val = srcs[ti][pl.ds(r + d_mid * row, PREP_ROWS, stride=d_mid), :]
                o_scr[ti, pl.ds(r * n_mid + row, PREP_ROWS), :] = val
                put(ti, 1, r, row, val)
        return carry

    lax.fori_loop(0, n_mid // PREP_ROWS, prep_mid, 0)

    n_big = seq // d_big
    for c in range(n_big // PREP_ROWS):
        for ti in range(3):
            for r in range(d_mid):
                for a in range(d_mid):
                    val = o_scr[ti, pl.ds(r * n_mid + a + d_mid * c * PREP_ROWS, PREP_ROWS, stride=d_mid), :]
                    put(ti, 2, r + d_mid * a, c * PREP_ROWS, val)

    for pi, (_, d) in enumerate(DILATED_PATTERNS):
        nblk = seq // d // BAND
        n_sub = seq // d
        qd, kd, vd = qd_scrs[pi], kd_scrs[pi], vd_scrs[pi]

        def group(it, carry, pi=pi, d=d, nblk=nblk, n_sub=n_sub, qd=qd, kd=kd, vd=vd):
            units = []
            for g in range(ATTN_GROUP):
                idx = it * ATTN_GROUP + g
                r = idx // nblk
                blk = idx % nblk
                q_row = pl.multiple_of(r * n_sub + blk * BAND, BAND)
                k_row = pl.multiple_of(r * (BAND + n_sub) + blk * BAND, BAND)
                out_rows = pl.ds(r + d * BAND * blk, BAND, stride=d) if d > 1 else pl.ds(blk * BAND, BAND)
                units.append((q_row, k_row, out_rows, (blk == 0).astype(jnp.int32)))

            for g, (q_row, k_row, _, _) in enumerate(units):
                q = qd[pl.ds(q_row, BAND), :]
                zero = jnp.zeros_like(q)
                q2 = jnp.concatenate([jnp.where(head0, q, zero), jnp.where(head0, zero, q)], axis=0)
                s_scr[g] = lax.dot_general(q2, kd[pl.ds(k_row, 2 * BAND), :], NT_DIMS,
                                           preferred_element_type=F32)

            for g, (_, _, _, first) in enumerate(units):
                for c in range(stacked // SOFTMAX_ROWS):
                    sl = pl.ds(c * SOFTMAX_ROWS, SOFTMAX_ROWS)
                    s = s_scr[g, sl, :] + bias_ref[pi, first, sl, :]
                    m = jnp.max(s, axis=-1, keepdims=True)
                    p = jnp.exp(s - m)
                    p_scr[g, sl, :] = p.astype(BF16)
                    mg_scr[g, sl, :] = jnp.broadcast_to(m, (SOFTMAX_ROWS, LANES))
                    dg_scr[g, sl, :] = jnp.broadcast_to(jnp.sum(p, axis=-1, keepdims=True), (SOFTMAX_ROWS, LANES))

            for g, (_, k_row, out_rows, _) in enumerate(units):
                o2 = jnp.dot(p_scr[g], vd[pl.ds(k_row, 2 * BAND), :], preferred_element_type=F32)
                o_scr[pi, out_rows, :] = jnp.where(head0, o2[:BAND], o2[BAND:])
                m_scr[pi, out_rows, :] = jnp.where(head0, mg_scr[g, pl.ds(0, BAND), :], mg_scr[g, pl.ds(BAND, BAND), :])
                d_scr[pi, out_rows, :] = jnp.where(head0, dg_scr[g, pl.ds(0, BAND), :], dg_scr[g, pl.ds(BAND, BAND), :])
            return carry

        lax.fori_loop(0, d * nblk // ATTN_GROUP, group, 0)

    chunk = 2 * BAND

    def combine(c, carry):
        sl = pl.ds(pl.multiple_of(c * chunk, chunk), chunk)
        ms = [m_scr[pi, sl, :] for pi in range(N_PATTERNS)]
        m = jnp.maximum(jnp.maximum(ms[0], ms[1]), ms[2])
        es = [jnp.exp(mi - m) for mi in ms]
        num = es[0] * o_scr[0, sl, :] + es[1] * o_scr[1, sl, :] + es[2] * o_scr[2, sl, :]
        den = es[0] * d_scr[0, sl, :] + es[1] * d_scr[1, sl, :] + es[2] * d_scr[2, sl, :]
        a_ref[sl, :] = num / den
        return carry

    lax.fori_loop(0, seq // chunk, combine, 0)


def _attn_prompt(z, bias, *, batch, seq, q_off, k_off, v_off, d_attn):
    n_pairs = d_attn // LANES
    stacked = HEADS_PER_TILE * BAND

    def col(off):
        return pl.BlockSpec((seq, LANES), lambda b, hp: (b, off // LANES + hp))

    return pl.pallas_call(
        functools.partial(_attn_prompt_kernel, seq=seq),
        out_shape=jax.ShapeDtypeStruct((batch * seq, d_attn), F32),
        grid=(batch, n_pairs),
        in_specs=[
            col(q_off), col(k_off), col(v_off),
            pl.BlockSpec((None, N_PATTERNS, 2, stacked, 2 * BAND), lambda b, hp: (hp, 0, 0, 0, 0)),
        ],
        out_specs=pl.BlockSpec((seq, LANES), lambda b, hp: (b, hp)),
        scratch_shapes=[pltpu.VMEM((N_PATTERNS, seq, LANES), F32),
                        pltpu.VMEM((N_PATTERNS, seq, LANES), F32),
                        pltpu.VMEM((N_PATTERNS, seq, LANES), F32),
                        [pltpu.VMEM((seq, LANES), BF16) for _ in DILATED_PATTERNS],
                        [pltpu.VMEM((seq + d * BAND, LANES), BF16) for _, d in DILATED_PATTERNS],
                        [pltpu.VMEM((seq + d * BAND, LANES), BF16) for _, d in DILATED_PATTERNS],
                        pltpu.VMEM((ATTN_GROUP, stacked, 2 * BAND), F32),
                        pltpu.VMEM((ATTN_GROUP, stacked, 2 * BAND), BF16),
                        pltpu.VMEM((ATTN_GROUP, stacked, LANES), F32),
                        pltpu.VMEM((ATTN_GROUP, stacked, LANES), F32)],
        compiler_params=_params("parallel", "parallel"),
        name="attn_prompt",
    )(z, z, z, bias)


def _pool_branch(ext_ref, u, pool_w_ref, scale, pos, *, halo, tm, group):
    parts = []
    for g, w in enumerate(POOL_WINDOWS):
        cols = slice(g * group, (g + 1) * group)
        acc = ext_ref[pl.ds(halo, tm), cols]
        for i in range(1, w):
            acc = acc + ext_ref[pl.ds(halo - i, tm), cols]
        cnt = jnp.minimum(w, pos + 1).astype(F32)
        pooled = acc / cnt - u[:, cols]
        parts.append(jnp.dot(pooled.astype(BF16), pool_w_ref[g], preferred_element_type=F32))
    return jnp.concatenate(parts, axis=-1) * scale


def _out_proj_kernel(x_ref, u_ref, halo_ref, gp_ref, ga_ref, a_ref, pw_ref, ps_ref, wo_ref, gf_ref,
                     o_ref, ext_ref, *, tm, tiles_per_seq, d_pool, final_norm):
    halo = halo_ref.shape[0]
    d_attn = a_ref.shape[1]
    tile = pl.program_id(0) % tiles_per_seq
    mix_a = (a_ref[...] * _silu(ga_ref[...])).astype(BF16)
    acc = x_ref[...] + jnp.dot(mix_a, wo_ref[pl.ds(d_pool, d_attn), :], preferred_element_type=F32)
    u = u_ref[...]
    ext_ref[pl.ds(0, halo), :] = jnp.where(tile == 0, 0.0, halo_ref[...])
    ext_ref[pl.ds(halo, tm), :] = u
    pos = tile * tm + lax.broadcasted_iota(jnp.int32, (tm, 1), 0)
    pm = _pool_branch(ext_ref, u, pw_ref, ps_ref[...], pos, halo=halo, tm=tm,
                      group=d_pool // len(POOL_WINDOWS))
    mix_p = (pm * _silu(gp_ref[...])).astype(BF16)
    acc = acc + jnp.dot(mix_p, wo_ref[pl.ds(0, d_pool), :], preferred_element_type=F32)
    if final_norm:
        acc = acc * lax.rsqrt(jnp.mean(acc * acc, axis=-1, keepdims=True) + RMS_EPS) * gf_ref[...]
    o_ref[...] = acc


def _out_proj(x, z, a, pool_w_bf16, pool_scale, w_out_bf16, layer, final_g, *, seq, tm, d_pool, ga_off,
              final_norm):
    m, d = x.shape
    halo = 2 * SUBLANES
    n_grp = len(POOL_WINDOWS)
    grp = d_pool // n_grp
    d_attn = a.shape[1]
    assert ga_off % d_attn == 0 and d_pool == d_attn and halo >= POOL_MAX - 1
    return pl.pallas_call(
        functools.partial(_out_proj_kernel, tm=tm, tiles_per_seq=seq // tm, d_pool=d_pool,
                          final_norm=final_norm),
        out_shape=jax.ShapeDtypeStruct((m, d), F32),
        grid=(m // tm,),
        in_specs=[
            pl.BlockSpec((tm, d), lambda i: (i, 0)),
            pl.BlockSpec((tm, d_pool), lambda i: (i, 0)),
            pl.BlockSpec((halo, d_pool), lambda i: (jnp.maximum(i * (tm // halo) - 1, 0), 0)),
            pl.BlockSpec((tm, d_pool), lambda i: (i, 1)),
            pl.BlockSpec((tm, d_attn), lambda i: (i, ga_off // d_attn)),
            pl.BlockSpec((tm, d_attn), lambda i: (i, 0)),
            pl.BlockSpec((None, n_grp, grp, grp), lambda i: (layer, 0, 0, 0)),
            pl.BlockSpec((1, d_pool), lambda i: (0, 0)),
            pl.BlockSpec((None, d_pool + d_attn, d), lambda i: (layer, 0, 0)),
            pl.BlockSpec((1, d), lambda i: (0, 0)),
        ],
        out_specs=pl.BlockSpec((tm, d), lambda i: (i, 0)),
        scratch_shapes=[pltpu.VMEM((halo + tm, d_pool), F32)],
        compiler_params=_params("parallel"),
        name="out_proj",
    )(x, z, z, z, z, a, pool_w_bf16, pool_scale.reshape(1, d_pool), w_out_bf16, final_g.reshape(1, d))


def _sample_bias_table(n_heads, l_win):
    slopes = _alibi_slopes(n_heads)
    dist = l_win - np.arange(l_win)
    table = np.empty((n_heads, SUBLANES, l_win), np.float32)
    for p, (window, d) in enumerate(DILATED_PATTERNS):
        valid = (dist % d == 0) & (dist <= window)
        table[:, p, :] = np.where(valid[None, :], -slopes[:, None] * dist[None, :].astype(np.float32), -np.inf)
    table[:, N_PATTERNS:, :] = table[:, :1, :]
    return jnp.asarray(table)


def _sample_mix_kernel(z_ref, st_ref, kt_ref, vt_ref, bias_ref, pw_ref, ps_ref,
                       mix_ref, st_out_ref, ext_ref, a_scr, *, d_pool, d_attn, pos0):
    n_heads = d_attn // HEAD_DIM
    u = z_ref[:, 0:d_pool]
    gp = z_ref[:, d_pool:2 * d_pool]
    ga = z_ref[:, 2 * d_pool + 3 * d_attn:2 * d_pool + 4 * d_attn]
    q_off, k_off, v_off = 2 * d_pool, 2 * d_pool + d_attn, 2 * d_pool + 2 * d_attn

    n_state = POOL_MAX - 1
    ext_ref[pl.ds(0, 1), :] = jnp.zeros((1, d_pool), F32)
    ext_ref[pl.ds(1, n_state), :] = st_ref[...]
    ext_ref[pl.ds(POOL_MAX, 1), :] = u
    pos = jnp.full((1, 1), pos0, jnp.int32)
    pm = _pool_branch(ext_ref, u, pw_ref, ps_ref[...], pos, halo=POOL_MAX, tm=1,
                      group=d_pool // len(POOL_WINDOWS))
    st_out_ref[pl.ds(0, n_state - 1), :] = st_ref[pl.ds(1, n_state - 1), :]
    st_out_ref[pl.ds(n_state - 1, 1), :] = u

    is_pattern = lax.broadcasted_iota(jnp.int32, (SUBLANES, 1), 0) < N_PATTERNS
    for h in range(n_heads):
        cols = slice(h * HEAD_DIM, (h + 1) * HEAD_DIM)

        def head_row(off, cols=cols):
            return z_ref[:, off:off + d_attn][:, cols]

        q8 = jnp.broadcast_to(head_row(q_off) * (HEAD_DIM ** -0.5), (SUBLANES, HEAD_DIM)).astype(BF16)
        k_new = head_row(k_off).astype(BF16).astype(F32)
        v_new = head_row(v_off).astype(BF16).astype(F32)
        s = jnp.dot(q8, kt_ref[h].astype(BF16), preferred_element_type=F32) + bias_ref[h]
        s_new = jnp.sum(q8.astype(F32) * k_new, axis=-1, keepdims=True)
        m = jnp.maximum(jnp.max(s, axis=-1, keepdims=True), s_new)
        p = jnp.exp(s - m)
        p_new = jnp.exp(s_new - m)
        den = jnp.sum(p, axis=-1, keepdims=True) + p_new
        o = lax.dot_general(p.astype(BF16), vt_ref[h].astype(BF16), NT_DIMS, preferred_element_type=F32)
        o = (o + p_new.astype(BF16).astype(F32) * v_new) / den
        lse = jnp.where(is_pattern, m + jnp.log(den), -jnp.inf)
        e = jnp.exp(lse - jnp.max(lse, axis=0, keepdims=True))
        a_scr[:, cols] = jnp.sum(e * o, axis=0, keepdims=True) / jnp.sum(e, axis=0, keepdims=True)

    mix_ref[:, :d_pool] = pm * _silu(gp)
    mix_ref[:, d_pool:] = a_scr[...] * _silu(ga)


def _sample_mix(z, state, kt_cache, vt_cache, layer, bias, pool_w_bf16, pool_scale, *, d_pool, d_attn, pos0):
    bd, n_state, _ = state.shape
    n_heads, _, l_win = kt_cache.shape[2:]
    n_in = z.shape[1]
    n_grp = len(POOL_WINDOWS)
    grp = d_pool // n_grp
    cache_spec = pl.BlockSpec((None, None, n_heads, HEAD_DIM, l_win), lambda b: (layer, b, 0, 0, 0))
    return pl.pallas_call(
        functools.partial(_sample_mix_kernel, d_pool=d_pool, d_attn=d_attn, pos0=pos0),
        out_shape=(jax.ShapeDtypeStruct((bd, 1, d_pool + d_attn), F32),
                   jax.ShapeDtypeStruct((bd, n_state, d_pool), F32)),
        grid=(bd,),
        in_specs=[
            pl.BlockSpec((None, 1, n_in), lambda b: (b, 0, 0)),
            pl.BlockSpec((None, n_state, d_pool), lambda b: (b, 0, 0)),
            cache_spec, cache_spec,
            pl.BlockSpec((n_heads, SUBLANES, l_win), lambda b: (0, 0, 0)),
            pl.BlockSpec((None, n_grp, grp, grp), lambda b: (layer, 0, 0, 0)),
            pl.BlockSpec((1, d_pool), lambda b: (0, 0)),
        ],
        out_specs=(pl.BlockSpec((None, 1, d_pool + d_attn), lambda b: (b, 0, 0)),
                   pl.BlockSpec((None, n_state, d_pool), lambda b: (b, 0, 0))),
        scratch_shapes=[pltpu.VMEM((POOL_MAX + SUBLANES, d_pool), F32),
                        pltpu.VMEM((1, d_attn), F32)],
        compiler_params=_params("parallel"),
        name="sample_mix",
    )(z.reshape(bd, 1, n_in), state, kt_cache, vt_cache, bias, pool_w_bf16,
      pool_scale.reshape(1, d_pool))


def _resid_matmul_kernel(x_ref, mix_ref, w_ref, o_ref):
    o_ref[...] = x_ref[...] + jnp.dot(mix_ref[...].astype(BF16), w_ref[...], preferred_element_type=F32)


def _resid_matmul(x, mix, w_bf16, layer, *, tn):
    m, d = x.shape
    kdim = mix.shape[1]
    return pl.pallas_call(
        _resid_matmul_kernel,
        out_shape=jax.ShapeDtypeStruct((m, d), F32),
        grid=(d // tn,),
        in_specs=[pl.BlockSpec((m, tn), lambda j: (0, j)),
                  pl.BlockSpec((m, kdim), lambda j: (0, 0)),
                  pl.BlockSpec((None, kdim, tn), lambda j: (layer, 0, j))],
        out_specs=pl.BlockSpec((m, tn), lambda j: (0, j)),
        compiler_params=_params("parallel"),
        name="sample_out_proj",
    )(x, mix, w_bf16)


def kernel(x_prompt, x_sample, state_pool, cache_k_win, cache_v_win, norm_g, w_in, pool_w, pool_scale,
           w_out, final_norm_g):
    batch, seq, d_model = x_prompt.shape
    bd, dec_seq, _ = x_sample.shape
    depth = norm_g.shape[0]
    d_pool = pool_scale.shape[1]
    d_attn = w_out.shape[1] - d_pool
    n_heads = d_attn // HEAD_DIM
    l_win = cache_k_win.shape[2]
    assert dec_seq == 1 and l_win == WIN_MAX and PAST_LEN + 1 >= POOL_MAX
    assert seq % (BAND * max(d for _, d in DILATED_PATTERNS)) == 0 and seq >= WIN_MAX
    q_off, k_off, v_off, ga_off = (2 * d_pool, 2 * d_pool + d_attn, 2 * d_pool + 2 * d_attn,
                                   2 * d_pool + 3 * d_attn)

    w_in_b = w_in.astype(BF16)
    w_out_b = w_out.astype(BF16)
    pool_w_b = pool_w.astype(BF16)
    bias_p = _prompt_bias_table(n_heads)
    bias_s = _sample_bias_table(n_heads, l_win)

    keep = min(WIN_MAX, seq)
    xp = x_prompt.reshape(batch * seq, d_model)
    pool_p, kt_p, vt_p = [], [], []
    for l in range(depth):
        z, kt, vt = _in_proj_window(xp, norm_g[l], w_in_b, l, seq=seq, keep=keep, k_off=k_off, v_off=v_off,
                                    d_attn=d_attn, tm=1024)
        a = _attn_prompt(z, bias_p, batch=batch, seq=seq, q_off=q_off, k_off=k_off, v_off=v_off, d_attn=d_attn)
        xp = _out_proj(xp, z, a, pool_w_b, pool_scale[l], w_out_b, l, final_norm_g, seq=seq, tm=256,
                       d_pool=d_pool, ga_off=ga_off, final_norm=(l == depth - 1))
        pool_p.append(z.reshape(batch, seq, -1)[:, seq - (POOL_MAX - 1):, :d_pool])
        kt_p.append(kt)
        vt_p.append(vt)
    y_prompt = xp.reshape(batch, seq, d_model)

    def from_slabs(slabs):
        return jnp.stack(slabs).reshape(depth, batch, n_heads, HEAD_DIM, keep).transpose(0, 1, 4, 2, 3)

    kt_cache = cache_k_win.transpose(0, 1, 3, 4, 2)
    vt_cache = cache_v_win.transpose(0, 1, 3, 4, 2)
    xs = x_sample.reshape(bd, d_model)
    pool_s, k_s, v_s = [], [], []
    for l in range(depth):
        z = _in_proj(xs, norm_g[l], w_in_b, l, tm=bd, tn=1024)
        mix, st = _sample_mix(z, state_pool[l], kt_cache, vt_cache, l, bias_s, pool_w_b, pool_scale[l],
                              d_pool=d_pool, d_attn=d_attn, pos0=PAST_LEN)
        xs = _resid_matmul(xs, mix.reshape(bd, -1), w_out_b, l, tn=1024)
        pool_s.append(st)
        k_s.append(z[:, k_off:k_off + d_attn].reshape(bd, 1, n_heads, HEAD_DIM))
        v_s.append(z[:, v_off:v_off + d_attn].reshape(bd, 1, n_heads, HEAD_DIM))
    y_sample = _rms_norm(xs, final_norm_g, tm=bd).reshape(bd, 1, d_model)

    return (y_prompt, y_sample,
            jnp.stack(pool_p), from_slabs(kt_p), from_slabs(vt_p),
            jnp.stack(pool_s), jnp.stack(k_s), jnp.stack(v_s))
```

```python
import functools

import numpy as np
import jax
import jax.numpy as jnp
from jax import lax
from jax.experimental import pallas as pl
from jax.experimental.pallas import tpu as pltpu

HEAD_DIM = 64
LANES = 128
SUBLANES = 8
HEADS_PER_TILE = LANES // HEAD_DIM
POOL_WINDOWS = (2, 4, 8, 16)
POOL_MAX = max(POOL_WINDOWS)
DILATED_PATTERNS = ((128, 1), (512, 4), (2048, 16))
N_PATTERNS = len(DILATED_PATTERNS)
BAND = 128
WIN_MAX = max(w for w, _ in DILATED_PATTERNS)
PAST_LEN = 16384
RMS_EPS = 1e-6
VMEM_LIMIT_BYTES = 56 * 2**20

F32 = jnp.float32
BF16 = jnp.bfloat16
NT_DIMS = (((1,), (1,)), ((), ()))


def _alibi_slopes(n_heads):
    return (2.0 ** (-8.0 * np.arange(1, n_heads + 1) / n_heads)).astype(np.float32)


def _silu(x):
    return x * (1.0 / (1.0 + jnp.exp(-x)))


def _params(*sem):
    return pltpu.CompilerParams(dimension_semantics=sem, vmem_limit_bytes=VMEM_LIMIT_BYTES)


def _normed_bf16(x_ref, g_ref, h_ref):
    x = x_ref[...]
    y = x * lax.rsqrt(jnp.mean(x * x, axis=-1, keepdims=True) + RMS_EPS)
    h_ref[...] = (y * g_ref[...]).astype(BF16)


def _in_proj_cast_kernel(x_ref, g_ref, w_ref, z_ref, wb_ref, h_ref):
    @pl.when(pl.program_id(0) == 0)
    def _():
        _normed_bf16(x_ref, g_ref, h_ref)

    wb_ref[...] = w_ref[...].astype(BF16)
    z_ref[...] = jnp.dot(h_ref[...], wb_ref[...], preferred_element_type=F32)


def _in_proj_window_kernel(x_ref, g_ref, w_ref, z_ref, kt_ref, vt_ref, h_ref,
                           *, tiles_per_seq, first_win_tile, k_tile, v_tile):
    j = pl.program_id(1)

    @pl.when(j == 0)
    def _():
        _normed_bf16(x_ref, g_ref, h_ref)

    z_ref[...] = jnp.dot(h_ref[...], w_ref[...], preferred_element_type=F32)
    in_window = pl.program_id(0) % tiles_per_seq >= first_win_tile

    @pl.when(jnp.logical_and(in_window, j == k_tile))
    def _():
        kt_ref[...] = z_ref[...].T

    @pl.when(jnp.logical_and(in_window, j == v_tile))
    def _():
        vt_ref[...] = z_ref[...].T


def _in_proj_cast(x, g, w_f32, layer, *, tn):
    m, d = x.shape
    n = w_f32.shape[2]
    return pl.pallas_call(
        _in_proj_cast_kernel,
        out_shape=(jax.ShapeDtypeStruct((m, n), F32), jax.ShapeDtypeStruct((d, n), BF16)),
        grid=(n // tn,),
        in_specs=[
            pl.BlockSpec((m, d), lambda j: (0, 0)),
            pl.BlockSpec((1, d), lambda j: (0, 0)),
            pl.BlockSpec((None, d, tn), lambda j: (layer, 0, j)),
        ],
        out_specs=(pl.BlockSpec((m, tn), lambda j: (0, j)),
                   pl.BlockSpec((d, tn), lambda j: (0, j))),
        scratch_shapes=[pltpu.VMEM((m, d), BF16)],
        compiler_params=_params("arbitrary"),
        name="in_proj_sample",
    )(x, g.reshape(1, d), w_f32)


def _in_proj_window(x, g, w_bf16, *, seq, keep, k_off, v_off, d_attn, tm):
    m, d = x.shape
    n = w_bf16.shape[1]
    tn = d_attn
    assert k_off % tn == 0 and v_off % tn == 0 and (seq - keep) % tm == 0 and seq % tm == 0
    tiles_per_seq = seq // tm
    first_win_tile = (seq - keep) // tm

    def win_map(i, j):
        return (i // tiles_per_seq, 0, jnp.maximum(i % tiles_per_seq - first_win_tile, 0))

    win = jax.ShapeDtypeStruct((m // seq, d_attn, keep), F32)
    return pl.pallas_call(
        functools.partial(_in_proj_window_kernel, tiles_per_seq=tiles_per_seq,
                          first_win_tile=first_win_tile, k_tile=k_off // tn, v_tile=v_off // tn),
        out_shape=(jax.ShapeDtypeStruct((m, n), F32), win, win),
        grid=(m // tm, n // tn),
        in_specs=[
            pl.BlockSpec((tm, d), lambda i, j: (i, 0)),
            pl.BlockSpec((1, d), lambda i, j: (0, 0)),
            pl.BlockSpec((d, tn), lambda i, j: (0, j)),
        ],
        out_specs=(pl.BlockSpec((tm, tn), lambda i, j: (i, j)),
                   pl.BlockSpec((None, d_attn, tm), win_map),
                   pl.BlockSpec((None, d_attn, tm), win_map)),
        scratch_shapes=[pltpu.VMEM((tm, d), BF16)],
        compiler_params=_params("arbitrary", "arbitrary"),
        name="in_proj_prompt",
    )(x, g.reshape(1, d), w_bf16)


def _rms_norm_kernel(x_ref, g_ref, o_ref):
    x = x_ref[...]
    y = x * lax.rsqrt(jnp.mean(x * x, axis=-1, keepdims=True) + RMS_EPS)
    o_ref[...] = y * g_ref[...]


def _rms_norm(x, g, *, tm):
    m, d = x.shape
    return pl.pallas_call(
        _rms_norm_kernel,
        out_shape=jax.ShapeDtypeStruct((m, d), F32),
        grid=(m // tm,),
        in_specs=[pl.BlockSpec((tm, d), lambda i: (i, 0)),
                  pl.BlockSpec((1, d), lambda i: (0, 0))],
        out_specs=pl.BlockSpec((tm, d), lambda i: (i, 0)),
        compiler_params=_params("parallel"),
        name="final_norm",
    )(x, g.reshape(1, d))


def _prompt_bias_table(n_heads):
    slopes = _alibi_slopes(n_heads)
    qi = np.arange(BAND)[:, None] + BAND
    kj = np.arange(2 * BAND)[None, :]
    dist = qi - kj
    valid = (dist >= 0) & (dist <= BAND)
    valid_first = valid & (kj >= BAND)
    table = np.empty((n_heads, N_PATTERNS, 2, BAND, 2 * BAND), np.float32)
    for h in range(n_heads):
        for p, (_, d) in enumerate(DILATED_PATTERNS):
            bias = -slopes[h] * (dist * d).astype(np.float32)
            table[h, p, 0] = np.where(valid, bias, -np.inf)
            table[h, p, 1] = np.where(valid_first, bias, -np.inf)
    table = table.reshape(n_heads // HEADS_PER_TILE, HEADS_PER_TILE, N_PATTERNS, 2, BAND, 2 * BAND)
    table = table.transpose(0, 2, 3, 1, 4, 5)
    return jnp.asarray(table.reshape(n_heads // HEADS_PER_TILE, N_PATTERNS, 2, HEADS_PER_TILE * BAND, 2 * BAND))


ATTN_GROUP = 4
SOFTMAX_ROWS = 32


PREP_ROWS = 256


def _attn_prompt_kernel(q_ref, k_ref, v_ref, bias_ref, a_ref,
                        o_scr, m_scr, d_scr, qd_scrs, kd_scrs, vd_scrs,
                        group_bufs, *, seq):
    lane = lax.broadcasted_iota(jnp.int32, (BAND, LANES), 1)
    head0 = lane < HEAD_DIM
    stacked = HEADS_PER_TILE * BAND
    scale = HEAD_DIM ** -0.5
    srcs = (q_ref, k_ref, v_ref)
    assert DILATED_PATTERNS[1][1] ** 2 == DILATED_PATTERNS[2][1] and DILATED_PATTERNS[0][1] == 1
    d_mid, d_big = DILATED_PATTERNS[1][1], DILATED_PATTERNS[2][1]

    def put(ti, pi, sub, row, val):
        n_sub = seq // DILATED_PATTERNS[pi][1]
        if ti == 0:
            qd_scrs[pi][pl.ds(sub * n_sub + row, PREP_ROWS), :] = (val * scale).astype(BF16)
        else:
            dst = (kd_scrs, vd_scrs)[ti - 1][pi]
            dst[pl.ds(sub * (BAND + n_sub) + BAND + row, PREP_ROWS), :] = val.astype(BF16)

    for pi, (_, d) in enumerate(DILATED_PATTERNS):
        n_sub = seq // d
        for dst in (kd_scrs[pi], vd_scrs[pi]):
            for r in range(d):
                dst[pl.ds(r * (BAND + n_sub), BAND), :] = jnp.zeros((BAND, LANES), BF16)

    def prep_dense(c, carry):
        row = pl.multiple_of(c * PREP_ROWS, PREP_ROWS)
        for ti in range(3):
            put(ti, 0, 0, row, srcs[ti][pl.ds(row, PREP_ROWS), :])
        return carry

    lax.fori_loop(0, seq // PREP_ROWS, prep_dense, 0)

    n_mid = seq // d_mid

    def prep_mid(c, carry):
        row = pl.multiple_of(c * PREP_ROWS, PREP_ROWS)
        for ti in range(3):
            for r in range(d_mid):
                val = srcs[ti][pl.ds(r + d_mid * row, PREP_ROWS, stride=d_mid), :]
                o_scr[ti, pl.ds(r * n_mid + row, PREP_ROWS), :] = val
                put(ti, 1, r, row, val)
        return carry

    lax.fori_loop(0, n_mid // PREP_ROWS, prep_mid, 0)

    n_big = seq // d_big
    for c in range(n_big // PREP_ROWS):
        for ti in range(3):
            for r in range(d_mid):
                for a in range(d_mid):
                    val = o_scr[ti, pl.ds(r * n_mid + a + d_mid * c * PREP_ROWS, PREP_ROWS, stride=d_mid), :]
                    put(ti, 2, r + d_mid * a, c * PREP_ROWS, val)

    for pi, (_, d) in enumerate(DILATED_PATTERNS):
        nblk = seq // d // BAND
        n_sub = seq // d
        qd, kd, vd = qd_scrs[pi], kd_scrs[pi], vd_scrs[pi]

        n_groups = d * nblk // ATTN_GROUP

        def units_of(grp, d=d, nblk=nblk, n_sub=n_sub):
            units = []
            for g in range(ATTN_GROUP):
                idx = grp * ATTN_GROUP + g
                r = idx // nblk
                blk = idx % nblk
                q_row = pl.multiple_of(r * n_sub + blk * BAND, BAND)
                k_row = pl.multiple_of(r * (BAND + n_sub) + blk * BAND, BAND)
                if d == d_big:
                    out_rows = pl.ds((r % d_mid) * n_mid + r // d_mid + d_mid * BAND * blk, BAND, stride=d_mid)
                else:
                    out_rows = pl.ds(q_row, BAND)
                units.append((q_row, k_row, out_rows, jnp.asarray(blk == 0, jnp.int32)))
            return units

        def scores(units, bufs, pi=pi, qd=qd, kd=kd):
            s_buf, _, mg_buf, _ = bufs
            for g, (q_row, k_row, _, first) in enumerate(units):
                q = qd[pl.ds(q_row, BAND), :]
                zero = jnp.zeros_like(q)
                q2 = jnp.concatenate([jnp.where(head0, q, zero), jnp.where(head0, zero, q)], axis=0)
                s = lax.dot_general(q2, kd[pl.ds(k_row, 2 * BAND), :], NT_DIMS,
                                    preferred_element_type=F32) + bias_ref[pi, first]
                s_buf[g] = s
                mg_buf[g] = jnp.broadcast_to(jnp.max(s, axis=-1, keepdims=True), (stacked, LANES))

        def softmax_values(units, bufs, pi=pi, vd=vd):
            s_buf, p_buf, mg_buf, dg_buf = bufs
            for g in range(ATTN_GROUP):
                for c in range(stacked // SOFTMAX_ROWS):
                    sl = pl.ds(c * SOFTMAX_ROWS, SOFTMAX_ROWS)
                    m = mg_buf[g, sl, :]
                    p = jnp.exp(s_buf[g, sl, :] - jnp.concatenate([m, m], axis=-1))
                    p_buf[g, sl, :] = p.astype(BF16)
                    dg_buf[g, sl, :] = jnp.broadcast_to(jnp.sum(p, axis=-1, keepdims=True), (SOFTMAX_ROWS, LANES))
            for g, (_, _, out_rows, _) in enumerate(units):
                m_scr[pi, out_rows, :] = jnp.where(head0, mg_buf[g, pl.ds(0, BAND), :], mg_buf[g, pl.ds(BAND, BAND), :])
                d_scr[pi, out_rows, :] = jnp.where(head0, dg_buf[g, pl.ds(0, BAND), :], dg_buf[g, pl.ds(BAND, BAND), :])
            for g, (_, k_row, out_rows, _) in enumerate(units):
                o2 = jnp.dot(p_buf[g], vd[pl.ds(k_row, 2 * BAND), :], preferred_element_type=F32)
                o_scr[pi, out_rows, :] = jnp.where(head0, o2[:BAND], o2[BAND:])

        def pair(j, carry, units_of=units_of, scores=scores, softmax_values=softmax_values, n_groups=n_groups):
            even, odd = units_of(2 * j), units_of(2 * j + 1)
            ahead = units_of(jnp.minimum(2 * j + 2, n_groups - 1))
            scores(odd, group_bufs[1])
            softmax_values(even, group_bufs[0])
            scores(ahead, group_bufs[0])
            softmax_values(odd, group_bufs[1])
            return carry

        scores(units_of(0), group_bufs[0])
        lax.fori_loop(0, n_groups // 2, pair, 0)

    chunk = 2 * BAND

    def combine(c, carry):
        start = pl.multiple_of(c * chunk, chunk)
        sl = pl.ds(start, chunk)
        natural = pl.ds(start // n_mid + d_mid * (start % n_mid), chunk, stride=d_mid)
        ms = [m_scr[0, natural, :], m_scr[1, sl, :], m_scr[2, sl, :]]
        m = jnp.maximum(jnp.maximum(ms[0], ms[1]), ms[2])
        es = [jnp.exp(mi - m) for mi in ms]
        num = es[0] * o_scr[0, natural, :] + es[1] * o_scr[1, sl, :] + es[2] * o_scr[2, sl, :]
        den = es[0] * d_scr[0, natural, :] + es[1] * d_scr[1, sl, :] + es[2] * d_scr[2, sl, :]
        a_ref[natural, :] = num / den
        return carry

    lax.fori_loop(0, seq // chunk, combine, 0)


def _attn_prompt(z, bias, *, batch, seq, q_off, k_off, v_off, d_attn):
    n_pairs = d_attn // LANES
    stacked = HEADS_PER_TILE * BAND

    def col(off):
        return pl.BlockSpec((seq, LANES), lambda b, hp: (b, off // LANES + hp))

    return pl.pallas_call(
        functools.partial(_attn_prompt_kernel, seq=seq),
        out_shape=jax.ShapeDtypeStruct((batch * seq, d_attn), F32),
        grid=(batch, n_pairs),
        in_specs=[
            col(q_off), col(k_off), col(v_off),
            pl.BlockSpec((None, N_PATTERNS, 2, stacked, 2 * BAND), lambda b, hp: (hp, 0, 0, 0, 0)),
        ],
        out_specs=pl.BlockSpec((seq, LANES), lambda b, hp: (b, hp)),
        scratch_shapes=[pltpu.VMEM((N_PATTERNS, seq, LANES), F32),
                        pltpu.VMEM((N_PATTERNS, seq, LANES), F32),
                        pltpu.VMEM((N_PATTERNS, seq, LANES), F32),
                        [pltpu.VMEM((seq, LANES), BF16) for _ in DILATED_PATTERNS],
                        [pltpu.VMEM((seq + d * BAND, LANES), BF16) for _, d in DILATED_PATTERNS],
                        [pltpu.VMEM((seq + d * BAND, LANES), BF16) for _, d in DILATED_PATTERNS],
                        [[pltpu.VMEM((ATTN_GROUP, stacked, 2 * BAND), F32),
                          pltpu.VMEM((ATTN_GROUP, stacked, 2 * BAND), BF16),
                          pltpu.VMEM((ATTN_GROUP, stacked, LANES), F32),
                          pltpu.VMEM((ATTN_GROUP, stacked, LANES), F32)]
                         for _ in range(2)]],
        compiler_params=_params("parallel", "parallel"),
        name="attn_prompt",
    )(z, z, z, bias)


def _pooled_window(ext_ref, row0, n_rows, cols, w, pos):
    cur = ext_ref[pl.ds(row0, n_rows), cols]
    acc = cur
    for i in range(1, w):
        acc = acc + ext_ref[pl.ds(row0 - i, n_rows), cols]
    return acc / jnp.minimum(w, pos + 1).astype(F32) - cur


POOL_ROWS = 64


def _prompt_mix(u_ref, halo_ref, gp_ref, ga_ref, a_ref, pw_ref, ps_ref, ext_ref, pooled_ref, mix_ref,
                *, tile, tm, d_pool):
    halo = halo_ref.shape[0]
    group = d_pool // len(POOL_WINDOWS)
    ext_ref[pl.ds(0, halo), :] = jnp.where(tile == 0, 0.0, halo_ref[...])
    ext_ref[pl.ds(halo, tm), :] = u_ref[...]
    for g, w in enumerate(POOL_WINDOWS):
        cols = slice(g * group, (g + 1) * group)
        for row in range(0, tm, POOL_ROWS):
            pos = tile * tm + row + lax.broadcasted_iota(jnp.int32, (POOL_ROWS, 1), 0)
            pooled = _pooled_window(ext_ref, halo + row, POOL_ROWS, cols, w, pos)
            pooled_ref[pl.ds(row, POOL_ROWS), cols] = pooled.astype(BF16)
        mixed = jnp.dot(pooled_ref[:, cols], pw_ref[g], preferred_element_type=F32) * ps_ref[:, cols]
        mix_ref[:, cols] = (mixed * _silu(gp_ref[:, cols])).astype(BF16)
    mix_ref[:, d_pool:] = (a_ref[...] * _silu(ga_ref[...])).astype(BF16)


def _out_proj_kernel(x_ref, u_ref, halo_ref, gp_ref, ga_ref, a_ref, pw_ref, ps_ref, wo_ref, gf_ref,
                     o_ref, ext_ref, pooled_ref, mix_ref, *, tm, tiles_per_seq, d_pool, final_norm):
    tile = pl.program_id(0) % tiles_per_seq
    _prompt_mix(u_ref, halo_ref, gp_ref, ga_ref, a_ref, pw_ref, ps_ref, ext_ref, pooled_ref, mix_ref,
                tile=tile, tm=tm, d_pool=d_pool)
    acc = x_ref[...] + jnp.dot(mix_ref[...], wo_ref[...], preferred_element_type=F32)
    if final_norm:
        acc = acc * lax.rsqrt(jnp.mean(acc * acc, axis=-1, keepdims=True) + RMS_EPS) * gf_ref[...]
    o_ref[...] = acc


def _out_proj(x, z, a, pool_w_bf16, pool_scale, w_out_bf16, layer, final_g, *, seq, tm, d_pool, ga_off,
              final_norm):
    m, d = x.shape
    halo = 2 * SUBLANES
    n_grp = len(POOL_WINDOWS)
    grp = d_pool // n_grp
    d_attn = a.shape[1]
    assert ga_off % d_attn == 0 and d_pool == d_attn and halo >= POOL_MAX - 1 and tm % POOL_ROWS == 0
    return pl.pallas_call(
        functools.partial(_out_proj_kernel, tm=tm, tiles_per_seq=seq // tm, d_pool=d_pool,
                          final_norm=final_norm),
        out_shape=jax.ShapeDtypeStruct((m, d), F32),
        grid=(m // tm,),
        in_specs=[
            pl.BlockSpec((tm, d), lambda i: (i, 0)),
            pl.BlockSpec((tm, d_pool), lambda i: (i, 0)),
            pl.BlockSpec((halo, d_pool), lambda i: (jnp.maximum(i * (tm // halo) - 1, 0), 0)),
            pl.BlockSpec((tm, d_pool), lambda i: (i, 1)),
            pl.BlockSpec((tm, d_attn), lambda i: (i, ga_off // d_attn)),
            pl.BlockSpec((tm, d_attn), lambda i: (i, 0)),
            pl.BlockSpec((None, n_grp, grp, grp), lambda i: (layer, 0, 0, 0)),
            pl.BlockSpec((1, d_pool), lambda i: (0, 0)),
            pl.BlockSpec((d_pool + d_attn, d), lambda i: (0, 0)),
            pl.BlockSpec((1, d), lambda i: (0, 0)),
        ],
        out_specs=pl.BlockSpec((tm, d), lambda i: (i, 0)),
        scratch_shapes=[pltpu.VMEM((halo + tm, d_pool), F32),
                        pltpu.VMEM((tm, d_pool), BF16),
                        pltpu.VMEM((tm, d_pool + d_attn), BF16)],
        compiler_params=_params("parallel"),
        name="out_proj",
    )(x, z, z, z, z, a, pool_w_bf16, pool_scale.reshape(1, d_pool), w_out_bf16, final_g.reshape(1, d))


def _sample_bias_table(n_heads, l_win):
    slopes = _alibi_slopes(n_heads)
    dist = l_win - np.arange(l_win)
    table = np.empty((n_heads, SUBLANES, l_win), np.float32)
    for p, (window, d) in enumerate(DILATED_PATTERNS):
        valid = (dist % d == 0) & (dist <= window)
        table[:, p, :] = np.where(valid[None, :], -slopes[:, None] * dist[None, :].astype(np.float32), -np.inf)
    table[:, N_PATTERNS:, :] = table[:, :1, :]
    return jnp.asarray(table)


def _sample_mix_kernel(z_ref, st_ref, kt_ref, vt_ref, bias_ref, pw_ref, ps_ref,
                       mix_ref, st_out_ref, ext_ref, a_scr, *, d_pool, d_attn, pos0):
    n_heads = d_attn // HEAD_DIM
    u = z_ref[:, 0:d_pool]
    gp = z_ref[:, d_pool:2 * d_pool]
    ga = z_ref[:, 2 * d_pool + 3 * d_attn:2 * d_pool + 4 * d_attn]
    q_off, k_off, v_off = 2 * d_pool, 2 * d_pool + d_attn, 2 * d_pool + 2 * d_attn

    n_state = POOL_MAX - 1
    ext_ref[pl.ds(0, 1), :] = jnp.zeros((1, d_pool), F32)
    ext_ref[pl.ds(1, n_state), :] = st_ref[...]
    ext_ref[pl.ds(POOL_MAX, 1), :] = u
    pos = jnp.full((1, 1), pos0, jnp.int32)
    group = d_pool // len(POOL_WINDOWS)
    parts = []
    for g, w in enumerate(POOL_WINDOWS):
        cols = slice(g * group, (g + 1) * group)
        pooled = _pooled_window(ext_ref, POOL_MAX, 1, cols, w, pos)
        parts.append(jnp.dot(pooled.astype(BF16), pw_ref[g], preferred_element_type=F32))
    pm = jnp.concatenate(parts, axis=-1) * ps_ref[...]
    st_out_ref[pl.ds(0, n_state - 1), :] = st_ref[pl.ds(1, n_state - 1), :]
    st_out_ref[pl.ds(n_state - 1, 1), :] = u

    is_pattern = lax.broadcasted_iota(jnp.int32, (SUBLANES, 1), 0) < N_PATTERNS
    for h in range(n_heads):
        cols = slice(h * HEAD_DIM, (h + 1) * HEAD_DIM)

        def head_row(off, cols=cols):
            return z_ref[:, off:off + d_attn][:, cols]

        q8 = jnp.broadcast_to(head_row(q_off) * (HEAD_DIM ** -0.5), (SUBLANES, HEAD_DIM)).astype(BF16)
        k_new = head_row(k_off).astype(BF16).astype(F32)
        v_new = head_row(v_off).astype(BF16).astype(F32)
        s = jnp.dot(q8, kt_ref[h].astype(BF16), preferred_element_type=F32) + bias_ref[h]
        s_new = jnp.sum(q8.astype(F32) * k_new, axis=-1, keepdims=True)
        m = jnp.maximum(jnp.max(s, axis=-1, keepdims=True), s_new)
        p = jnp.exp(s - m)
        p_new = jnp.exp(s_new - m)
        den = jnp.sum(p, axis=-1, keepdims=True) + p_new
        o = lax.dot_general(p.astype(BF16), vt_ref[h].astype(BF16), NT_DIMS, preferred_element_type=F32)
        o = (o + p_new.astype(BF16).astype(F32) * v_new) / den
        lse = jnp.where(is_pattern, m + jnp.log(den), -jnp.inf)
        e = jnp.exp(lse - jnp.max(lse, axis=0, keepdims=True))
        a_scr[:, cols] = jnp.sum(e * o, axis=0, keepdims=True) / jnp.sum(e, axis=0, keepdims=True)

    mix_ref[:, :d_pool] = pm * _silu(gp)
    mix_ref[:, d_pool:] = a_scr[...] * _silu(ga)


def _sample_mix(z, state, kt_cache, vt_cache, layer, bias, pool_w_bf16, pool_scale, *, d_pool, d_attn, pos0):
    bd, n_state, _ = state.shape
    n_heads, _, l_win = kt_cache.shape[2:]
    n_in = z.shape[1]
    n_grp = len(POOL_WINDOWS)
    grp = d_pool // n_grp
    cache_spec = pl.BlockSpec((None, None, n_heads, HEAD_DIM, l_win), lambda b: (layer, b, 0, 0, 0))
    return pl.pallas_call(
        functools.partial(_sample_mix_kernel, d_pool=d_pool, d_attn=d_attn, pos0=pos0),
        out_shape=(jax.ShapeDtypeStruct((bd, 1, d_pool + d_attn), F32),
                   jax.ShapeDtypeStruct((bd, n_state, d_pool), F32)),
        grid=(bd,),
        in_specs=[
            pl.BlockSpec((None, 1, n_in), lambda b: (b, 0, 0)),
            pl.BlockSpec((None, n_state, d_pool), lambda b: (b, 0, 0)),
            cache_spec, cache_spec,
            pl.BlockSpec((n_heads, SUBLANES, l_win), lambda b: (0, 0, 0)),
            pl.BlockSpec((None, n_grp, grp, grp), lambda b: (layer, 0, 0, 0)),
            pl.BlockSpec((1, d_pool), lambda b: (0, 0)),
        ],
        out_specs=(pl.BlockSpec((None, 1, d_pool + d_attn), lambda b: (b, 0, 0)),
                   pl.BlockSpec((None, n_state, d_pool), lambda b: (b, 0, 0))),
        scratch_shapes=[pltpu.VMEM((POOL_MAX + SUBLANES, d_pool), F32),
                        pltpu.VMEM((1, d_attn), F32)],
        compiler_params=_params("parallel"),
        name="sample_mix",
    )(z.reshape(bd, 1, n_in), state, kt_cache, vt_cache, bias, pool_w_bf16,
      pool_scale.reshape(1, d_pool))


def _resid_matmul_cast_kernel(x_ref, mix_ref, w_ref, o_ref, wb_ref):
    wb_ref[...] = w_ref[...].astype(BF16)
    o_ref[...] = x_ref[...] + jnp.dot(mix_ref[...].astype(BF16), wb_ref[...], preferred_element_type=F32)


def _resid_matmul_cast(x, mix, w_f32, layer, *, tn):
    m, d = x.shape
    kdim = mix.shape[1]
    return pl.pallas_call(
        _resid_matmul_cast_kernel,
        out_shape=(jax.ShapeDtypeStruct((m, d), F32), jax.ShapeDtypeStruct((kdim, d), BF16)),
        grid=(d // tn,),
        in_specs=[pl.BlockSpec((m, tn), lambda j: (0, j)),
                  pl.BlockSpec((m, kdim), lambda j: (0, 0)),
                  pl.BlockSpec((None, kdim, tn), lambda j: (layer, 0, j))],
        out_specs=(pl.BlockSpec((m, tn), lambda j: (0, j)),
                   pl.BlockSpec((kdim, tn), lambda j: (0, j))),
        compiler_params=_params("parallel"),
        name="sample_out_proj",
    )(x, mix, w_f32)


def kernel(x_prompt, x_sample, state_pool, cache_k_win, cache_v_win, norm_g, w_in, pool_w, pool_scale,
           w_out, final_norm_g):
    batch, seq, d_model = x_prompt.shape
    bd, dec_seq, _ = x_sample.shape
    depth = norm_g.shape[0]
    d_pool = pool_scale.shape[1]
    d_attn = w_out.shape[1] - d_pool
    n_heads = d_attn // HEAD_DIM
    l_win = cache_k_win.shape[2]
    assert dec_seq == 1 and l_win == WIN_MAX and PAST_LEN + 1 >= POOL_MAX
    assert seq % (BAND * max(d for _, d in DILATED_PATTERNS)) == 0 and seq >= WIN_MAX
    q_off, k_off, v_off, ga_off = (2 * d_pool, 2 * d_pool + d_attn, 2 * d_pool + 2 * d_attn,
                                   2 * d_pool + 3 * d_attn)

    pool_w_b = pool_w.astype(BF16)
    bias_p = _prompt_bias_table(n_heads)
    bias_s = _sample_bias_table(n_heads, l_win)

    kt_cache = cache_k_win.transpose(0, 1, 3, 4, 2)
    vt_cache = cache_v_win.transpose(0, 1, 3, 4, 2)
    xs = x_sample.reshape(bd, d_model)
    pool_s, k_s, v_s, w_in_b, w_out_b = [], [], [], [], []
    for l in range(depth):
        z, wb = _in_proj_cast(xs, norm_g[l], w_in, l, tn=1024)
        w_in_b.append(wb)
        mix, st = _sample_mix(z, state_pool[l], kt_cache, vt_cache, l, bias_s, pool_w_b, pool_scale[l],
                              d_pool=d_pool, d_attn=d_attn, pos0=PAST_LEN)
        xs, wb = _resid_matmul_cast(xs, mix.reshape(bd, -1), w_out, l, tn=1024)
        w_out_b.append(wb)
        pool_s.append(st)
        k_s.append(z[:, k_off:k_off + d_attn].reshape(bd, 1, n_heads, HEAD_DIM))
        v_s.append(z[:, v_off:v_off + d_attn].reshape(bd, 1, n_heads, HEAD_DIM))
    y_sample = _rms_norm(xs, final_norm_g, tm=bd).reshape(bd, 1, d_model)

    keep = min(WIN_MAX, seq)
    xp = x_prompt.reshape(batch * seq, d_model)
    pool_p, kt_p, vt_p = [], [], []
    for l in range(depth):
        z, kt, vt = _in_proj_window(xp, norm_g[l], w_in_b[l], seq=seq, keep=keep, k_off=k_off, v_off=v_off,
                                    d_attn=d_attn, tm=1024)
        a = _attn_prompt(z, bias_p, batch=batch, seq=seq, q_off=q_off, k_off=k_off, v_off=v_off, d_attn=d_attn)
        xp = _out_proj(xp, z, a, pool_w_b, pool_scale[l], w_out_b[l], l, final_norm_g, seq=seq, tm=256,
                       d_pool=d_pool, ga_off=ga_off, final_norm=(l == depth - 1))
        pool_p.append(z.reshape(batch, seq, -1)[:, seq - (POOL_MAX - 1):, :d_pool])
        kt_p.append(kt)
        vt_p.append(vt)
    y_prompt = xp.reshape(batch, seq, d_model)

    def from_slabs(slabs):
        return jnp.stack(slabs).reshape(depth, batch, n_heads, HEAD_DIM, keep).transpose(0, 1, 4, 2, 3)

    return (y_prompt, y_sample,
            jnp.stack(pool_p), from_slabs(kt_p), from_slabs(vt_p),
            jnp.stack(pool_s), jnp.stack(k_s), jnp.stack(v_s))
```

```python
import functools

import numpy as np
import jax
import jax.numpy as jnp
from jax import lax
from jax.experimental import pallas as pl
from jax.experimental.pallas import tpu as pltpu

HEAD_DIM = 64
LANES = 128
SUBLANES = 8
HEADS_PER_TILE = LANES // HEAD_DIM
POOL_WINDOWS = (2, 4, 8, 16)
POOL_MAX = max(POOL_WINDOWS)
DILATED_PATTERNS = ((128, 1), (512, 4), (2048, 16))
N_PATTERNS = len(DILATED_PATTERNS)
BAND = 128
WIN_MAX = max(w for w, _ in DILATED_PATTERNS)
PAST_LEN = 16384
RMS_EPS = 1e-6
VMEM_LIMIT_BYTES = 56 * 2**20

F32 = jnp.float32
BF16 = jnp.bfloat16
NT_DIMS = (((1,), (1,)), ((), ()))


def _alibi_slopes(n_heads):
    return (2.0 ** (-8.0 * np.arange(1, n_heads + 1) / n_heads)).astype(np.float32)


def _silu(x):
    return x * (1.0 / (1.0 + jnp.exp(-x)))


def _params(*sem):
    return pltpu.CompilerParams(dimension_semantics=sem, vmem_limit_bytes=VMEM_LIMIT_BYTES)


def _normed_bf16(x_ref, g_ref, h_ref):
    x = x_ref[...]
    y = x * lax.rsqrt(jnp.mean(x * x, axis=-1, keepdims=True) + RMS_EPS)
    h_ref[...] = (y * g_ref[...]).astype(BF16)


def _in_proj_cast_kernel(x_ref, g_ref, w_ref, z_ref, wb_ref, h_ref):
    @pl.when(pl.program_id(0) == 0)
    def _():
        _normed_bf16(x_ref, g_ref, h_ref)

    wb_ref[...] = w_ref[...].astype(BF16)
    z_ref[...] = jnp.dot(h_ref[...], wb_ref[...], preferred_element_type=F32)


def _in_proj_window_kernel(x_ref, g_ref, w_ref, z_ref, kt_ref, vt_ref, h_ref,
                           *, tiles_per_seq, first_win_tile, k_tile, v_tile):
    j = pl.program_id(1)

    @pl.when(j == 0)
    def _():
        _normed_bf16(x_ref, g_ref, h_ref)

    z_ref[...] = jnp.dot(h_ref[...], w_ref[...], preferred_element_type=F32)
    in_window = pl.program_id(0) % tiles_per_seq >= first_win_tile

    @pl.when(jnp.logical_and(in_window, j == k_tile))
    def _():
        kt_ref[...] = z_ref[...].T

    @pl.when(jnp.logical_and(in_window, j == v_tile))
    def _():
        vt_ref[...] = z_ref[...].T


def _in_proj_cast(x, g, w_f32, layer, *, tn):
    m, d = x.shape
    n = w_f32.shape[2]
    return pl.pallas_call(
        _in_proj_cast_kernel,
        out_shape=(jax.ShapeDtypeStruct((m, n), F32), jax.ShapeDtypeStruct((d, n), BF16)),
        grid=(n // tn,),
        in_specs=[
            pl.BlockSpec((m, d), lambda j: (0, 0)),
            pl.BlockSpec((1, d), lambda j: (0, 0)),
            pl.BlockSpec((None, d, tn), lambda j: (layer, 0, j)),
        ],
        out_specs=(pl.BlockSpec((m, tn), lambda j: (0, j)),
                   pl.BlockSpec((d, tn), lambda j: (0, j))),
        scratch_shapes=[pltpu.VMEM((m, d), BF16)],
        compiler_params=_params("arbitrary"),
        name="in_proj_sample",
    )(x, g.reshape(1, d), w_f32)


def _in_proj_window(x, g, w_bf16, *, seq, keep, k_off, v_off, d_attn, tm):
    m, d = x.shape
    n = w_bf16.shape[1]
    tn = d_attn
    assert k_off % tn == 0 and v_off % tn == 0 and (seq - keep) % tm == 0 and seq % tm == 0
    tiles_per_seq = seq // tm
    first_win_tile = (seq - keep) // tm

    def win_map(i, j):
        return (i // tiles_per_seq, 0, jnp.maximum(i % tiles_per_seq - first_win_tile, 0))

    win = jax.ShapeDtypeStruct((m // seq, d_attn, keep), F32)
    return pl.pallas_call(
        functools.partial(_in_proj_window_kernel, tiles_per_seq=tiles_per_seq,
                          first_win_tile=first_win_tile, k_tile=k_off // tn, v_tile=v_off // tn),
        out_shape=(jax.ShapeDtypeStruct((m, n), F32), win, win),
        grid=(m // tm, n // tn),
        in_specs=[
            pl.BlockSpec((tm, d), lambda i, j: (i, 0)),
            pl.BlockSpec((1, d), lambda i, j: (0, 0)),
            pl.BlockSpec((d, tn), lambda i, j: (0, j)),
        ],
        out_specs=(pl.BlockSpec((tm, tn), lambda i, j: (i, j)),
                   pl.BlockSpec((None, d_attn, tm), win_map),
                   pl.BlockSpec((None, d_attn, tm), win_map)),
        scratch_shapes=[pltpu.VMEM((tm, d), BF16)],
        compiler_params=_params("arbitrary", "arbitrary"),
        name="in_proj_prompt",
    )(x, g.reshape(1, d), w_bf16)


def _rms_norm_kernel(x_ref, g_ref, o_ref):
    x = x_ref[...]
    y = x * lax.rsqrt(jnp.mean(x * x, axis=-1, keepdims=True) + RMS_EPS)
    o_ref[...] = y * g_ref[...]


def _rms_norm(x, g, *, tm):
    m, d = x.shape
    return pl.pallas_call(
        _rms_norm_kernel,
        out_shape=jax.ShapeDtypeStruct((m, d), F32),
        grid=(m // tm,),
        in_specs=[pl.BlockSpec((tm, d), lambda i: (i, 0)),
                  pl.BlockSpec((1, d), lambda i: (0, 0))],
        out_specs=pl.BlockSpec((tm, d), lambda i: (i, 0)),
        compiler_params=_params("parallel"),
        name="final_norm",
    )(x, g.reshape(1, d))


def _prompt_bias_table(n_heads):
    slopes = _alibi_slopes(n_heads)
    qi = np.arange(BAND)[:, None] + BAND
    kj = np.arange(2 * BAND)[None, :]
    dist = qi - kj
    valid = (dist >= 0) & (dist <= BAND)
    valid_first = valid & (kj >= BAND)
    table = np.empty((n_heads, N_PATTERNS, 2, BAND, 2 * BAND), np.float32)
    for h in range(n_heads):
        for p, (_, d) in enumerate(DILATED_PATTERNS):
            bias = -slopes[h] * (dist * d).astype(np.float32)
            table[h, p, 0] = np.where(valid, bias, -np.inf)
            table[h, p, 1] = np.where(valid_first, bias, -np.inf)
    table = table.reshape(n_heads // HEADS_PER_TILE, HEADS_PER_TILE, N_PATTERNS, 2, BAND, 2 * BAND)
    table = table.transpose(0, 2, 3, 1, 4, 5)
    return jnp.asarray(table.reshape(n_heads // HEADS_PER_TILE, N_PATTERNS, 2, HEADS_PER_TILE * BAND, 2 * BAND))


ATTN_GROUP = 4
SOFTMAX_ROWS = 32


PREP_ROWS = 256


def _attn_prompt_kernel(q_ref, k_ref, v_ref, bias_ref, a_ref,
                        o_scr, m_scr, d_scr, qd_scrs, kd_scrs, vd_scrs,
                        group_bufs, *, seq):
    lane = lax.broadcasted_iota(jnp.int32, (BAND, LANES), 1)
    head0 = lane < HEAD_DIM
    stacked = HEADS_PER_TILE * BAND
    scale = HEAD_DIM ** -0.5
    srcs = (q_ref, k_ref, v_ref)
    assert DILATED_PATTERNS[1][1] ** 2 == DILATED_PATTERNS[2][1] and DILATED_PATTERNS[0][1] == 1
    d_mid, d_big = DILATED_PATTERNS[1][1], DILATED_PATTERNS[2][1]

    def put(ti, pi, sub, row, val):
        n_sub = seq // DILATED_PATTERNS[pi][1]
        if ti == 0:
            qd_scrs[pi][pl.ds(sub * n_sub + row, PREP_ROWS), :] = (val * scale).astype(BF16)
        else:
            dst = (kd_scrs, vd_scrs)[ti - 1][pi]
            dst[pl.ds(sub * (BAND + n_sub) + BAND + row, PREP_ROWS), :] = val.astype(BF16)

    for pi, (_, d) in enumerate(DILATED_PATTERNS):
        n_sub = seq // d
        for dst in (kd_scrs[pi], vd_scrs[pi]):
            for r in range(d):
                dst[pl.ds(r * (BAND + n_sub), BAND), :] = jnp.zeros((BAND, LANES), BF16)

    def prep_dense(c, carry):
        row = pl.multiple_of(c * PREP_ROWS, PREP_ROWS)
        for ti in range(3):
            put(ti, 0, 0, row, srcs[ti][pl.ds(row, PREP_ROWS), :])
        return carry

    lax.fori_loop(0, seq // PREP_ROWS, prep_dense, 0)

    n_mid = seq // d_mid

    def prep_mid(c, carry):
        row = pl.multiple_of(c * PREP_ROWS, PREP_ROWS)
        for ti in range(3):
            for r in range(d_mid):
                val = srcs[ti][pl.ds(r + d_mid * row, PREP_ROWS, stride=d_mid), :]
                o_scr[ti, pl.ds(r * n_mid + row, PREP_ROWS), :] = val
                put(ti, 1, r, row, val)
        return carry

    lax.fori_loop(0, n_mid // PREP_ROWS, prep_mid, 0)

    n_big = seq // d_big
    for c in range(n_big // PREP_ROWS):
        for ti in range(3):
            for r in range(d_mid):
                for a in range(d_mid):
                    val = o_scr[ti, pl.ds(r * n_mid + a + d_mid * c * PREP_ROWS, PREP_ROWS, stride=d_mid), :]
                    put(ti, 2, r + d_mid * a, c * PREP_ROWS, val)

    def aligned(row):
        return row if isinstance(row, int) else pl.multiple_of(row, BAND)

    stages = []
    for pi, (_, d) in enumerate(DILATED_PATTERNS):
        nblk = seq // d // BAND
        n_sub = seq // d
        qd, kd, vd = qd_scrs[pi], kd_scrs[pi], vd_scrs[pi]

        def units_of(grp, d=d, nblk=nblk, n_sub=n_sub):
            units = []
            for g in range(ATTN_GROUP):
                idx = grp * ATTN_GROUP + g
                r = idx // nblk
                blk = idx % nblk
                q_row = aligned(r * n_sub + blk * BAND)
                k_row = aligned(r * (BAND + n_sub) + blk * BAND)
                if d == d_big:
                    out_rows = pl.ds((r % d_mid) * n_mid + r // d_mid + d_mid * BAND * blk, BAND, stride=d_mid)
                else:
                    out_rows = pl.ds(q_row, BAND)
                first = int(blk == 0) if isinstance(blk, int) else jnp.asarray(blk == 0, jnp.int32)
                units.append((q_row, k_row, out_rows, first))
            return units

        def scores(units, bufs, pi=pi, qd=qd, kd=kd):
            s_buf, _, mg_buf, _ = bufs
            for g, (q_row, k_row, _, first) in enumerate(units):
                q = qd[pl.ds(q_row, BAND), :]
                zero = jnp.zeros_like(q)
                q2 = jnp.concatenate([jnp.where(head0, q, zero), jnp.where(head0, zero, q)], axis=0)
                s = lax.dot_general(q2, kd[pl.ds(k_row, 2 * BAND), :], NT_DIMS,
                                    preferred_element_type=F32) + bias_ref[pi, first]
                s_buf[g] = s
                mg_buf[g] = jnp.broadcast_to(jnp.max(s, axis=-1, keepdims=True), (stacked, LANES))

        def softmax_values(units, bufs, pi=pi, vd=vd):
            s_buf, p_buf, mg_buf, dg_buf = bufs
            for g in range(ATTN_GROUP):
                for c in range(stacked // SOFTMAX_ROWS):
                    sl = pl.ds(c * SOFTMAX_ROWS, SOFTMAX_ROWS)
                    m = mg_buf[g, sl, :]
                    p = jnp.exp(s_buf[g, sl, :] - jnp.concatenate([m, m], axis=-1))
                    p_buf[g, sl, :] = p.astype(BF16)
                    dg_buf[g, sl, :] = jnp.broadcast_to(jnp.sum(p, axis=-1, keepdims=True), (SOFTMAX_ROWS, LANES))
            for g, (_, _, out_rows, _) in enumerate(units):
                m_scr[pi, out_rows, :] = jnp.where(head0, mg_buf[g, pl.ds(0, BAND), :], mg_buf[g, pl.ds(BAND, BAND), :])
                d_scr[pi, out_rows, :] = jnp.where(head0, dg_buf[g, pl.ds(0, BAND), :], dg_buf[g, pl.ds(BAND, BAND), :])
            for g, (_, k_row, out_rows, _) in enumerate(units):
                o2 = jnp.dot(p_buf[g], vd[pl.ds(k_row, 2 * BAND), :], preferred_element_type=F32)
                o_scr[pi, out_rows, :] = jnp.where(head0, o2[:BAND], o2[BAND:])

        stages.append((units_of, scores, softmax_values, d * nblk // ATTN_GROUP))

    def pair(j, units_of, scores, softmax_values, score_ahead):
        even, odd = units_of(2 * j), units_of(2 * j + 1)
        scores(odd, group_bufs[1])
        softmax_values(even, group_bufs[0])
        score_ahead()
        softmax_values(odd, group_bufs[1])

    stages[0][1](stages[0][0](0), group_bufs[0])
    for pi, (units_of, scores, softmax_values, n_groups) in enumerate(stages):
        def body(j, carry, units_of=units_of, scores=scores, softmax_values=softmax_values):
            pair(j, units_of, scores, softmax_values, lambda: scores(units_of(2 * j + 2), group_bufs[0]))
            return carry

        lax.fori_loop(0, n_groups // 2 - 1, body, 0)
        if pi + 1 < len(stages):
            next_units_of, next_scores = stages[pi + 1][:2]
            score_ahead = functools.partial(next_scores, next_units_of(0), group_bufs[0])
        else:
            score_ahead = lambda: None
        pair(n_groups // 2 - 1, units_of, scores, softmax_values, score_ahead)

    chunk = 2 * BAND

    def combine(c, carry):
        start = pl.multiple_of(c * chunk, chunk)
        sl = pl.ds(start, chunk)
        natural = pl.ds(start // n_mid + d_mid * (start % n_mid), chunk, stride=d_mid)
        ms = [m_scr[0, natural, :], m_scr[1, sl, :], m_scr[2, sl, :]]
        m = jnp.maximum(jnp.maximum(ms[0], ms[1]), ms[2])
        es = [jnp.exp(mi - m) for mi in ms]
        num = es[0] * o_scr[0, natural, :] + es[1] * o_scr[1, sl, :] + es[2] * o_scr[2, sl, :]
        den = es[0] * d_scr[0, natural, :] + es[1] * d_scr[1, sl, :] + es[2] * d_scr[2, sl, :]
        a_ref[natural, :] = num / den
        return carry

    lax.fori_loop(0, seq // chunk, combine, 0)


def _attn_prompt(z, bias, *, batch, seq, q_off, k_off, v_off, d_attn):
    n_pairs = d_attn // LANES
    stacked = HEADS_PER_TILE * BAND

    def col(off):
        return pl.BlockSpec((seq, LANES), lambda b, hp: (b, off // LANES + hp))

    return pl.pallas_call(
        functools.partial(_attn_prompt_kernel, seq=seq),
        out_shape=jax.ShapeDtypeStruct((batch * seq, d_attn), F32),
        grid=(batch, n_pairs),
        in_specs=[
            col(q_off), col(k_off), col(v_off),
            pl.BlockSpec((None, N_PATTERNS, 2, stacked, 2 * BAND), lambda b, hp: (hp, 0, 0, 0, 0)),
        ],
        out_specs=pl.BlockSpec((seq, LANES), lambda b, hp: (b, hp)),
        scratch_shapes=[pltpu.VMEM((N_PATTERNS, seq, LANES), F32),
                        pltpu.VMEM((N_PATTERNS, seq, LANES), F32),
                        pltpu.VMEM((N_PATTERNS, seq, LANES), F32),
                        [pltpu.VMEM((seq, LANES), BF16) for _ in DILATED_PATTERNS],
                        [pltpu.VMEM((seq + d * BAND, LANES), BF16) for _, d in DILATED_PATTERNS],
                        [pltpu.VMEM((seq + d * BAND, LANES), BF16) for _, d in DILATED_PATTERNS],
                        [[pltpu.VMEM((ATTN_GROUP, stacked, 2 * BAND), F32),
                          pltpu.VMEM((ATTN_GROUP, stacked, 2 * BAND), BF16),
                          pltpu.VMEM((ATTN_GROUP, stacked, LANES), F32),
                          pltpu.VMEM((ATTN_GROUP, stacked, LANES), F32)]
                         for _ in range(2)]],
        compiler_params=_params("parallel", "parallel"),
        name="attn_prompt",
    )(z, z, z, bias)


def _pooled_window(ext_ref, row0, n_rows, cols, w, pos):
    cur = ext_ref[pl.ds(row0, n_rows), cols]
    acc = cur
    for i in range(1, w):
        acc = acc + ext_ref[pl.ds(row0 - i, n_rows), cols]
    return acc / jnp.minimum(w, pos + 1).astype(F32) - cur


POOL_ROWS = 64


def _prompt_mix(u_ref, halo_ref, gp_ref, ga_ref, a_ref, pw_ref, ps_ref, ext_ref, pooled_ref, mix_ref,
                *, tile, tm, d_pool):
    halo = halo_ref.shape[0]
    group = d_pool // len(POOL_WINDOWS)
    ext_ref[pl.ds(0, halo), :] = jnp.where(tile == 0, 0.0, halo_ref[...])
    ext_ref[pl.ds(halo, tm), :] = u_ref[...]
    for g, w in enumerate(POOL_WINDOWS):
        cols = slice(g * group, (g + 1) * group)
        for row in range(0, tm, POOL_ROWS):
            pos = tile * tm + row + lax.broadcasted_iota(jnp.int32, (POOL_ROWS, 1), 0)
            pooled = _pooled_window(ext_ref, halo + row, POOL_ROWS, cols, w, pos)
            pooled_ref[pl.ds(row, POOL_ROWS), cols] = pooled.astype(BF16)
        mixed = jnp.dot(pooled_ref[:, cols], pw_ref[g], preferred_element_type=F32) * ps_ref[:, cols]
        mix_ref[:, cols] = (mixed * _silu(gp_ref[:, cols])).astype(BF16)
    mix_ref[:, d_pool:] = (a_ref[...] * _silu(ga_ref[...])).astype(BF16)


def _out_proj_kernel(x_ref, u_ref, halo_ref, gp_ref, ga_ref, a_ref, pw_ref, ps_ref, wo_ref, gf_ref,
                     o_ref, ext_ref, pooled_ref, mix_ref, *, tm, tiles_per_seq, d_pool, final_norm):
    tile = pl.program_id(0) % tiles_per_seq
    _prompt_mix(u_ref, halo_ref, gp_ref, ga_ref, a_ref, pw_ref, ps_ref, ext_ref, pooled_ref, mix_ref,
                tile=tile, tm=tm, d_pool=d_pool)
    acc = x_ref[...] + jnp.dot(mix_ref[...], wo_ref[...], preferred_element_type=F32)
    if final_norm:
        acc = acc * lax.rsqrt(jnp.mean(acc * acc, axis=-1, keepdims=True) + RMS_EPS) * gf_ref[...]
    o_ref[...] = acc


def _out_proj(x, z, a, pool_w_bf16, pool_scale, w_out_bf16, layer, final_g, *, seq, tm, d_pool, ga_off,
              final_norm):
    m, d = x.shape
    halo = 2 * SUBLANES
    n_grp = len(POOL_WINDOWS)
    grp = d_pool // n_grp
    d_attn = a.shape[1]
    assert ga_off % d_attn == 0 and d_pool == d_attn and halo >= POOL_MAX - 1 and tm % POOL_ROWS == 0
    return pl.pallas_call(
        functools.partial(_out_proj_kernel, tm=tm, tiles_per_seq=seq // tm, d_pool=d_pool,
                          final_norm=final_norm),
        out_shape=jax.ShapeDtypeStruct((m, d), F32),
        grid=(m // tm,),
        in_specs=[
            pl.BlockSpec((tm, d), lambda i: (i, 0)),
            pl.BlockSpec((tm, d_pool), lambda i: (i, 0)),
            pl.BlockSpec((halo, d_pool), lambda i: (jnp.maximum(i * (tm // halo) - 1, 0), 0)),
            pl.BlockSpec((tm, d_pool), lambda i: (i, 1)),
            pl.BlockSpec((tm, d_attn), lambda i: (i, ga_off // d_attn)),
            pl.BlockSpec((tm, d_attn), lambda i: (i, 0)),
            pl.BlockSpec((None, n_grp, grp, grp), lambda i: (layer, 0, 0, 0)),
            pl.BlockSpec((1, d_pool), lambda i: (0, 0)),
            pl.BlockSpec((d_pool + d_attn, d), lambda i: (0, 0), pipeline_mode=pl.Buffered(1)),
            pl.BlockSpec((1, d), lambda i: (0, 0)),
        ],
        out_specs=pl.BlockSpec((tm, d), lambda i: (i, 0)),
        scratch_shapes=[pltpu.VMEM((halo + tm, d_pool), F32),
                        pltpu.VMEM((tm, d_pool), BF16),
                        pltpu.VMEM((tm, d_pool + d_attn), BF16)],
        compiler_params=_params("parallel"),
        name="out_proj",
    )(x, z, z, z, z, a, pool_w_bf16, pool_scale.reshape(1, d_pool), w_out_bf16, final_g.reshape(1, d))


def _sample_bias_table(n_heads, l_win):
    slopes = _alibi_slopes(n_heads)
    dist = l_win - np.arange(l_win)
    table = np.empty((n_heads, SUBLANES, l_win), np.float32)
    for p, (window, d) in enumerate(DILATED_PATTERNS):
        valid = (dist % d == 0) & (dist <= window)
        table[:, p, :] = np.where(valid[None, :], -slopes[:, None] * dist[None, :].astype(np.float32), -np.inf)
    table[:, N_PATTERNS:, :] = table[:, :1, :]
    return jnp.asarray(table)


def _sample_mix_kernel(z_ref, st_ref, kt_ref, vt_ref, bias_ref, pw_ref, ps_ref,
                       mix_ref, st_out_ref, ext_ref, a_scr, *, d_pool, d_attn, pos0):
    n_heads = d_attn // HEAD_DIM
    u = z_ref[:, 0:d_pool]
    gp = z_ref[:, d_pool:2 * d_pool]
    ga = z_ref[:, 2 * d_pool + 3 * d_attn:2 * d_pool + 4 * d_attn]
    q_off, k_off, v_off = 2 * d_pool, 2 * d_pool + d_attn, 2 * d_pool + 2 * d_attn

    n_state = POOL_MAX - 1
    ext_ref[pl.ds(0, 1), :] = jnp.zeros((1, d_pool), F32)
    ext_ref[pl.ds(1, n_state), :] = st_ref[...]
    ext_ref[pl.ds(POOL_MAX, 1), :] = u
    pos = jnp.full((1, 1), pos0, jnp.int32)
    group = d_pool // len(POOL_WINDOWS)
    parts = []
    for g, w in enumerate(POOL_WINDOWS):
        cols = slice(g * group, (g + 1) * group)
        pooled = _pooled_window(ext_ref, POOL_MAX, 1, cols, w, pos)
        parts.append(jnp.dot(pooled.astype(BF16), pw_ref[g], preferred_element_type=F32))
    pm = jnp.concatenate(parts, axis=-1) * ps_ref[...]
    st_out_ref[pl.ds(0, n_state - 1), :] = st_ref[pl.ds(1, n_state - 1), :]
    st_out_ref[pl.ds(n_state - 1, 1), :] = u

    is_pattern = lax.broadcasted_iota(jnp.int32, (SUBLANES, 1), 0) < N_PATTERNS
    for h in range(n_heads):
        cols = slice(h * HEAD_DIM, (h + 1) * HEAD_DIM)

        def head_row(off, cols=cols):
            return z_ref[:, off:off + d_attn][:, cols]

        q8 = jnp.broadcast_to(head_row(q_off) * (HEAD_DIM ** -0.5), (SUBLANES, HEAD_DIM)).astype(BF16)
        k_new = head_row(k_off).astype(BF16).astype(F32)
        v_new = head_row(v_off).astype(BF16).astype(F32)
        s = jnp.dot(q8, kt_ref[h].astype(BF16), preferred_element_type=F32) + bias_ref[h]
        s_new = jnp.sum(q8.astype(F32) * k_new, axis=-1, keepdims=True)
        m = jnp.maximum(jnp.max(s, axis=-1, keepdims=True), s_new)
        p = jnp.exp(s - m)
        p_new = jnp.exp(s_new - m)
        den = jnp.sum(p, axis=-1, keepdims=True) + p_new
        o = lax.dot_general(p.astype(BF16), vt_ref[h].astype(BF16), NT_DIMS, preferred_element_type=F32)
        o = (o + p_new.astype(BF16).astype(F32) * v_new) / den
        lse = jnp.where(is_pattern, m + jnp.log(den), -jnp.inf)
        e = jnp.exp(lse - jnp.max(lse, axis=0, keepdims=True))
        a_scr[:, cols] = jnp.sum(e * o, axis=0, keepdims=True) / jnp.sum(e, axis=0, keepdims=True)

    mix_ref[:, :d_pool] = pm * _silu(gp)
    mix_ref[:, d_pool:] = a_scr[...] * _silu(ga)


def _sample_mix(z, state, kt_cache, vt_cache, layer, bias, pool_w_bf16, pool_scale, *, d_pool, d_attn, pos0):
    bd, n_state, _ = state.shape
    n_heads, _, l_win = kt_cache.shape[2:]
    n_in = z.shape[1]
    n_grp = len(POOL_WINDOWS)
    grp = d_pool // n_grp
    cache_spec = pl.BlockSpec((None, None, n_heads, HEAD_DIM, l_win), lambda b: (layer, b, 0, 0, 0))
    return pl.pallas_call(
        functools.partial(_sample_mix_kernel, d_pool=d_pool, d_attn=d_attn, pos0=pos0),
        out_shape=(jax.ShapeDtypeStruct((bd, 1, d_pool + d_attn), F32),
                   jax.ShapeDtypeStruct((bd, n_state, d_pool), F32)),
        grid=(bd,),
        in_specs=[
            pl.BlockSpec((None, 1, n_in), lambda b: (b, 0, 0)),
            pl.BlockSpec((None, n_state, d_pool), lambda b: (b, 0, 0)),
            cache_spec, cache_spec,
            pl.BlockSpec((n_heads, SUBLANES, l_win), lambda b: (0, 0, 0)),
            pl.BlockSpec((None, n_grp, grp, grp), lambda b: (layer, 0, 0, 0)),
            pl.BlockSpec((1, d_pool), lambda b: (0, 0)),
        ],
        out_specs=(pl.BlockSpec((None, 1, d_pool + d_attn), lambda b: (b, 0, 0)),
                   pl.BlockSpec((None, n_state, d_pool), lambda b: (b, 0, 0))),
        scratch_shapes=[pltpu.VMEM((POOL_MAX + SUBLANES, d_pool), F32),
                        pltpu.VMEM((1, d_attn), F32)],
        compiler_params=_params("parallel"),
        name="sample_mix",
    )(z.reshape(bd, 1, n_in), state, kt_cache, vt_cache, bias, pool_w_bf16,
      pool_scale.reshape(1, d_pool))


def _resid_matmul_cast_kernel(x_ref, mix_ref, w_ref, o_ref, wb_ref):
    wb_ref[...] = w_ref[...].astype(BF16)
    o_ref[...] = x_ref[...] + jnp.dot(mix_ref[...].astype(BF16), wb_ref[...], preferred_element_type=F32)


def _resid_matmul_cast(x, mix, w_f32, layer, *, tn):
    m, d = x.shape
    kdim = mix.shape[1]
    return pl.pallas_call(
        _resid_matmul_cast_kernel,
        out_shape=(jax.ShapeDtypeStruct((m, d), F32), jax.ShapeDtypeStruct((kdim, d), BF16)),
        grid=(d // tn,),
        in_specs=[pl.BlockSpec((m, tn), lambda j: (0, j)),
                  pl.BlockSpec((m, kdim), lambda j: (0, 0)),
                  pl.BlockSpec((None, kdim, tn), lambda j: (layer, 0, j))],
        out_specs=(pl.BlockSpec((m, tn), lambda j: (0, j)),
                   pl.BlockSpec((kdim, tn), lambda j: (0, j))),
        compiler_params=_params("parallel"),
        name="sample_out_proj",
    )(x, mix, w_f32)


def kernel(x_prompt, x_sample, state_pool, cache_k_win, cache_v_win, norm_g, w_in, pool_w, pool_scale,
           w_out, final_norm_g):
    batch, seq, d_model = x_prompt.shape
    bd, dec_seq, _ = x_sample.shape
    depth = norm_g.shape[0]
    d_pool = pool_scale.shape[1]
    d_attn = w_out.shape[1] - d_pool
    n_heads = d_attn // HEAD_DIM
    l_win = cache_k_win.shape[2]
    assert dec_seq == 1 and l_win == WIN_MAX and PAST_LEN + 1 >= POOL_MAX
    assert seq % (BAND * max(d for _, d in DILATED_PATTERNS)) == 0 and seq >= WIN_MAX
    q_off, k_off, v_off, ga_off = (2 * d_pool, 2 * d_pool + d_attn, 2 * d_pool + 2 * d_attn,
                                   2 * d_pool + 3 * d_attn)

    pool_w_b = pool_w.astype(BF16)
    bias_p = _prompt_bias_table(n_heads)
    bias_s = _sample_bias_table(n_heads, l_win)

    kt_cache = cache_k_win.transpose(0, 1, 3, 4, 2)
    vt_cache = cache_v_win.transpose(0, 1, 3, 4, 2)
    xs = x_sample.reshape(bd, d_model)
    pool_s, k_s, v_s, w_in_b, w_out_b = [], [], [], [], []
    for l in range(depth):
        z, wb = _in_proj_cast(xs, norm_g[l], w_in, l, tn=1024)
        w_in_b.append(wb)
        mix, st = _sample_mix(z, state_pool[l], kt_cache, vt_cache, l, bias_s, pool_w_b, pool_scale[l],
                              d_pool=d_pool, d_attn=d_attn, pos0=PAST_LEN)
        xs, wb = _resid_matmul_cast(xs, mix.reshape(bd, -1), w_out, l, tn=1024)
        w_out_b.append(wb)
        pool_s.append(st)
        k_s.append(z[:, k_off:k_off + d_attn].reshape(bd, 1, n_heads, HEAD_DIM))
        v_s.append(z[:, v_off:v_off + d_attn].reshape(bd, 1, n_heads, HEAD_DIM))
    y_sample = _rms_norm(xs, final_norm_g, tm=bd).reshape(bd, 1, d_model)

    keep = min(WIN_MAX, seq)
    xp = x_prompt.reshape(batch * seq, d_model)
    pool_p, kt_p, vt_p = [], [], []
    for l in range(depth):
        z, kt, vt = _in_proj_window(xp, norm_g[l], w_in_b[l], seq=seq, keep=keep, k_off=k_off, v_off=v_off,
                                    d_attn=d_attn, tm=1024)
        a = _attn_prompt(z, bias_p, batch=batch, seq=seq, q_off=q_off, k_off=k_off, v_off=v_off, d_attn=d_attn)
        xp = _out_proj(xp, z, a, pool_w_b, pool_scale[l], w_out_b[l], l, final_norm_g, seq=seq, tm=512,
                       d_pool=d_pool, ga_off=ga_off, final_norm=(l == depth - 1))
        pool_p.append(z.reshape(batch, seq, -1)[:, seq - (POOL_MAX - 1):, :d_pool])
        kt_p.append(kt)
        vt_p.append(vt)
    y_prompt = xp.reshape(batch, seq, d_model)

    def from_slabs(slabs):
        return jnp.stack(slabs).reshape(depth, batch, n_heads, HEAD_DIM, keep).transpose(0, 1, 4, 2, 3)

    return (y_prompt, y_sample,
            jnp.stack(pool_p), from_slabs(kt_p), from_slabs(vt_p),
            jnp.stack(pool_s), jnp.stack(k_s), jnp.stack(v_s))
```

```python
import functools

import numpy as np
import jax
import jax.numpy as jnp
from jax import lax
from jax.experimental import pallas as pl
from jax.experimental.pallas import tpu as pltpu

HEAD_DIM = 64
LANES = 128
SUBLANES = 8
HEADS_PER_TILE = LANES // HEAD_DIM
POOL_WINDOWS = (2, 4, 8, 16)
POOL_MAX = max(POOL_WINDOWS)
DILATED_PATTERNS = ((128, 1), (512, 4), (2048, 16))
N_PATTERNS = len(DILATED_PATTERNS)
BAND = 128
WIN_MAX = max(w for w, _ in DILATED_PATTERNS)
PAST_LEN = 16384
RMS_EPS = 1e-6
VMEM_LIMIT_BYTES = 56 * 2**20

F32 = jnp.float32
BF16 = jnp.bfloat16
NT_DIMS = (((1,), (1,)), ((), ()))


def _alibi_slopes(n_heads):
    return (2.0 ** (-8.0 * np.arange(1, n_heads + 1) / n_heads)).astype(np.float32)


def _silu(x):
    return x * (1.0 / (1.0 + jnp.exp(-x)))


def _params(*sem):
    return pltpu.CompilerParams(dimension_semantics=sem, vmem_limit_bytes=VMEM_LIMIT_BYTES)


def _normed_bf16(x_ref, g_ref, h_ref):
    x = x_ref[...]
    y = x * lax.rsqrt(jnp.mean(x * x, axis=-1, keepdims=True) + RMS_EPS)
    h_ref[...] = (y * g_ref[...]).astype(BF16)


def _in_proj_cast_kernel(x_ref, g_ref, w_ref, z_ref, wb_ref, h_ref):
    @pl.when(pl.program_id(0) == 0)
    def _():
        _normed_bf16(x_ref, g_ref, h_ref)

    wb_ref[...] = w_ref[...].astype(BF16)
    z_ref[...] = jnp.dot(h_ref[...], wb_ref[...], preferred_element_type=F32)


def _norm_cast_kernel(x_ref, g_ref, h_ref):
    _normed_bf16(x_ref, g_ref, h_ref)


def _norm_cast(x, g, *, tm):
    m, d = x.shape
    return pl.pallas_call(
        _norm_cast_kernel,
        out_shape=jax.ShapeDtypeStruct((m, d), BF16),
        grid=(m // tm,),
        in_specs=[pl.BlockSpec((tm, d), lambda i: (i, 0)),
                  pl.BlockSpec((1, d), lambda i: (0, 0))],
        out_specs=pl.BlockSpec((tm, d), lambda i: (i, 0)),
        compiler_params=_params("parallel"),
        name="norm_cast",
    )(x, g.reshape(1, d))


def _in_proj_window_kernel(h_ref, w_ref, z_ref, kt_ref, vt_ref,
                           *, tiles_per_seq, first_win_tile, k_col, v_col, d_attn):
    j = pl.program_id(1)
    z_ref[...] = jnp.dot(h_ref[...], w_ref[...], preferred_element_type=F32)
    in_window = pl.program_id(0) % tiles_per_seq >= first_win_tile

    @pl.when(jnp.logical_and(in_window, j == k_col[0]))
    def _():
        kt_ref[...] = z_ref[:, k_col[1]:k_col[1] + d_attn].T

    @pl.when(jnp.logical_and(in_window, j == v_col[0]))
    def _():
        vt_ref[...] = z_ref[:, v_col[1]:v_col[1] + d_attn].T


def _in_proj_cast(x, g, w_f32, layer, *, tn):
    m, d = x.shape
    n = w_f32.shape[2]
    return pl.pallas_call(
        _in_proj_cast_kernel,
        out_shape=(jax.ShapeDtypeStruct((m, n), F32), jax.ShapeDtypeStruct((d, n), BF16)),
        grid=(n // tn,),
        in_specs=[
            pl.BlockSpec((m, d), lambda j: (0, 0)),
            pl.BlockSpec((1, d), lambda j: (0, 0)),
            pl.BlockSpec((None, d, tn), lambda j: (layer, 0, j)),
        ],
        out_specs=(pl.BlockSpec((m, tn), lambda j: (0, j)),
                   pl.BlockSpec((d, tn), lambda j: (0, j))),
        scratch_shapes=[pltpu.VMEM((m, d), BF16)],
        compiler_params=_params("arbitrary"),
        name="in_proj_sample",
    )(x, g.reshape(1, d), w_f32)


def _in_proj_window(h, w_bf16, *, seq, keep, k_off, v_off, d_attn, tm, tn):
    m, d = h.shape
    n = w_bf16.shape[1]
    assert k_off // tn == (k_off + d_attn - 1) // tn and v_off // tn == (v_off + d_attn - 1) // tn
    assert (seq - keep) % tm == 0 and seq % tm == 0
    tiles_per_seq = seq // tm
    first_win_tile = (seq - keep) // tm

    def win_map(i, j):
        return (i // tiles_per_seq, 0, jnp.maximum(i % tiles_per_seq - first_win_tile, 0))

    win = jax.ShapeDtypeStruct((m // seq, d_attn, keep), F32)
    win_spec = pl.BlockSpec((None, d_attn, tm), win_map, pipeline_mode=pl.Buffered(1))
    return pl.pallas_call(
        functools.partial(_in_proj_window_kernel, tiles_per_seq=tiles_per_seq, first_win_tile=first_win_tile,
                          k_col=(k_off // tn, k_off % tn), v_col=(v_off // tn, v_off % tn), d_attn=d_attn),
        out_shape=(jax.ShapeDtypeStruct((m, n), F32), win, win),
        grid=(m // tm, n // tn),
        in_specs=[
            pl.BlockSpec((tm, d), lambda i, j: (i, 0)),
            pl.BlockSpec((d, tn), lambda i, j: (0, j)),
        ],
        out_specs=(pl.BlockSpec((tm, tn), lambda i, j: (i, j)), win_spec, win_spec),
        compiler_params=_params("arbitrary", "arbitrary"),
        name="in_proj_prompt",
    )(h, w_bf16)


def _rms_norm_kernel(x_ref, g_ref, o_ref):
    x = x_ref[...]
    y = x * lax.rsqrt(jnp.mean(x * x, axis=-1, keepdims=True) + RMS_EPS)
    o_ref[...] = y * g_ref[...]


def _rms_norm(x, g, *, tm):
    m, d = x.shape
    return pl.pallas_call(
        _rms_norm_kernel,
        out_shape=jax.ShapeDtypeStruct((m, d), F32),
        grid=(m // tm,),
        in_specs=[pl.BlockSpec((tm, d), lambda i: (i, 0)),
                  pl.BlockSpec((1, d), lambda i: (0, 0))],
        out_specs=pl.BlockSpec((tm, d), lambda i: (i, 0)),
        compiler_params=_params("parallel"),
        name="final_norm",
    )(x, g.reshape(1, d))


def _prompt_bias_table(n_heads):
    slopes = _alibi_slopes(n_heads)
    qi = np.arange(BAND)[:, None] + BAND
    kj = np.arange(2 * BAND)[None, :]
    dist = qi - kj
    valid = (dist >= 0) & (dist <= BAND)
    valid_first = valid & (kj >= BAND)
    table = np.empty((n_heads, N_PATTERNS, 2, BAND, 2 * BAND), np.float32)
    for h in range(n_heads):
        for p, (_, d) in enumerate(DILATED_PATTERNS):
            bias = -slopes[h] * (dist * d).astype(np.float32)
            table[h, p, 0] = np.where(valid, bias, -np.inf)
            table[h, p, 1] = np.where(valid_first, bias, -np.inf)
    table = table.reshape(n_heads // HEADS_PER_TILE, HEADS_PER_TILE, N_PATTERNS, 2, BAND, 2 * BAND)
    table = table.transpose(0, 2, 3, 1, 4, 5)
    return jnp.asarray(table.reshape(n_heads // HEADS_PER_TILE, N_PATTERNS, 2, HEADS_PER_TILE * BAND, 2 * BAND))


ATTN_GROUP = 4
SOFTMAX_ROWS = 32


PREP_ROWS = 256


def _attn_prompt_kernel(q_ref, k_ref, v_ref, bias_ref, a_ref,
                        o_scr, m_scr, d_scr, qd_scrs, kd_scrs, vd_scrs,
                        group_bufs, *, seq):
    lane = lax.broadcasted_iota(jnp.int32, (BAND, LANES), 1)
    head0 = lane < HEAD_DIM
    stacked = HEADS_PER_TILE * BAND
    scale = HEAD_DIM ** -0.5
    srcs = (q_ref, k_ref, v_ref)
    assert DILATED_PATTERNS[1][1] ** 2 == DILATED_PATTERNS[2][1] and DILATED_PATTERNS[0][1] == 1
    d_mid, d_big = DILATED_PATTERNS[1][1], DILATED_PATTERNS[2][1]

    def put(ti, pi, sub, row, val):
        n_sub = seq // DILATED_PATTERNS[pi][1]
        if ti == 0:
            qd_scrs[pi][pl.ds(sub * n_sub + row, PREP_ROWS), :] = (val * scale).astype(BF16)
        else:
            dst = (kd_scrs, vd_scrs)[ti - 1][pi]
            dst[pl.ds(sub * (BAND + n_sub) + BAND + row, PREP_ROWS), :] = val.astype(BF16)

    for pi, (_, d) in enumerate(DILATED_PATTERNS):
        n_sub = seq // d
        for dst in (kd_scrs[pi], vd_scrs[pi]):
            for r in range(d):
                dst[pl.ds(r * (BAND + n_sub), BAND), :] = jnp.zeros((BAND, LANES), BF16)

    def prep_dense(c, carry):
        row = pl.multiple_of(c * PREP_ROWS, PREP_ROWS)
        for ti in range(3):
            put(ti, 0, 0, row, srcs[ti][pl.ds(row, PREP_ROWS), :])
        return carry

    lax.fori_loop(0, seq // PREP_ROWS, prep_dense, 0)

    n_mid = seq // d_mid

    def prep_mid(c, carry):
        row = pl.multiple_of(c * PREP_ROWS, PREP_ROWS)
        for ti in range(3):
            for r in range(d_mid):
                val = srcs[ti][pl.ds(r + d_mid * row, PREP_ROWS, stride=d_mid), :]
                o_scr[ti, pl.ds(r * n_mid + row, PREP_ROWS), :] = val
                put(ti, 1, r, row, val)
        return carry

    lax.fori_loop(0, n_mid // PREP_ROWS, prep_mid, 0)

    n_big = seq // d_big
    for c in range(n_big // PREP_ROWS):
        for ti in range(3):
            for r in range(d_mid):
                for a in range(d_mid):
                    val = o_scr[ti, pl.ds(r * n_mid + a + d_mid * c * PREP_ROWS, PREP_ROWS, stride=d_mid), :]
                    put(ti, 2, r + d_mid * a, c * PREP_ROWS, val)

    def aligned(row):
        return row if isinstance(row, int) else pl.multiple_of(row, BAND)

    stages = []
    for pi, (_, d) in enumerate(DILATED_PATTERNS):
        nblk = seq // d // BAND
        n_sub = seq // d
        qd, kd, vd = qd_scrs[pi], kd_scrs[pi], vd_scrs[pi]

        def units_of(grp, d=d, nblk=nblk, n_sub=n_sub):
            units = []
            for g in range(ATTN_GROUP):
                idx = grp * ATTN_GROUP + g
                r = idx // nblk
                blk = idx % nblk
                q_row = aligned(r * n_sub + blk * BAND)
                k_row = aligned(r * (BAND + n_sub) + blk * BAND)
                if d == d_big:
                    out_rows = pl.ds((r % d_mid) * n_mid + r // d_mid + d_mid * BAND * blk, BAND, stride=d_mid)
                else:
                    out_rows = pl.ds(q_row, BAND)
                first = int(blk == 0) if isinstance(blk, int) else jnp.asarray(blk == 0, jnp.int32)
                units.append((q_row, k_row, out_rows, first))
            return units

        def scores(units, bufs, pi=pi, qd=qd, kd=kd):
            s_buf, _, mg_buf, _ = bufs
            for g, (q_row, k_row, _, first) in enumerate(units):
                q = qd[pl.ds(q_row, BAND), :]
                zero = jnp.zeros_like(q)
                q2 = jnp.concatenate([jnp.where(head0, q, zero), jnp.where(head0, zero, q)], axis=0)
                s = lax.dot_general(q2, kd[pl.ds(k_row, 2 * BAND), :], NT_DIMS,
                                    preferred_element_type=F32) + bias_ref[pi, first]
                s_buf[g] = s
                mg_buf[g] = jnp.broadcast_to(jnp.max(s, axis=-1, keepdims=True), (stacked, LANES))

        def softmax_values(units, bufs, pi=pi, vd=vd):
            s_buf, p_buf, mg_buf, dg_buf = bufs
            for g in range(ATTN_GROUP):
                for c in range(stacked // SOFTMAX_ROWS):
                    sl = pl.ds(c * SOFTMAX_ROWS, SOFTMAX_ROWS)
                    m = mg_buf[g, sl, :]
                    p = jnp.exp(s_buf[g, sl, :] - jnp.concatenate([m, m], axis=-1))
                    p_buf[g, sl, :] = p.astype(BF16)
                    dg_buf[g, sl, :] = jnp.broadcast_to(jnp.sum(p, axis=-1, keepdims=True), (SOFTMAX_ROWS, LANES))
            for g, (_, _, out_rows, _) in enumerate(units):
                m_scr[pi, out_rows, :] = jnp.where(head0, mg_buf[g, pl.ds(0, BAND), :], mg_buf[g, pl.ds(BAND, BAND), :])
                d_scr[pi, out_rows, :] = jnp.where(head0, dg_buf[g, pl.ds(0, BAND), :], dg_buf[g, pl.ds(BAND, BAND), :])
            for g, (_, k_row, out_rows, _) in enumerate(units):
                o2 = jnp.dot(p_buf[g], vd[pl.ds(k_row, 2 * BAND), :], preferred_element_type=F32)
                o_scr[pi, out_rows, :] = jnp.where(head0, o2[:BAND], o2[BAND:])

        stages.append((units_of, scores, softmax_values, d * nblk // ATTN_GROUP))

    def pair(j, units_of, scores, softmax_values, score_ahead):
        even, odd = units_of(2 * j), units_of(2 * j + 1)
        scores(odd, group_bufs[1])
        softmax_values(even, group_bufs[0])
        score_ahead()
        softmax_values(odd, group_bufs[1])

    stages[0][1](stages[0][0](0), group_bufs[0])
    for pi, (units_of, scores, softmax_values, n_groups) in enumerate(stages):
        def body(j, carry, units_of=units_of, scores=scores, softmax_values=softmax_values):
            pair(j, units_of, scores, softmax_values, lambda: scores(units_of(2 * j + 2), group_bufs[0]))
            return carry

        lax.fori_loop(0, n_groups // 2 - 1, body, 0)
        if pi + 1 < len(stages):
            next_units_of, next_scores = stages[pi + 1][:2]
            score_ahead = functools.partial(next_scores, next_units_of(0), group_bufs[0])
        else:
            score_ahead = lambda: None
        pair(n_groups // 2 - 1, units_of, scores, softmax_values, score_ahead)

    chunk = 2 * BAND

    def combine(c, carry):
        start = pl.multiple_of(c * chunk, chunk)
        sl = pl.ds(start, chunk)
        natural = pl.ds(start // n_mid + d_mid * (start % n_mid), chunk, stride=d_mid)
        ms = [m_scr[0, natural, :], m_scr[1, sl, :], m_scr[2, sl, :]]
        m = jnp.maximum(jnp.maximum(ms[0], ms[1]), ms[2])
        es = [jnp.exp(mi - m) for mi in ms]
        num = es[0] * o_scr[0, natural, :] + es[1] * o_scr[1, sl, :] + es[2] * o_scr[2, sl, :]
        den = es[0] * d_scr[0, natural, :] + es[1] * d_scr[1, sl, :] + es[2] * d_scr[2, sl, :]
        a_ref[natural, :] = num / den
        return carry

    lax.fori_loop(0, seq // chunk, combine, 0)


def _attn_prompt(z, bias, *, batch, seq, q_off, k_off, v_off, d_attn):
    n_pairs = d_attn // LANES
    stacked = HEADS_PER_TILE * BAND

    def col(off):
        return pl.BlockSpec((seq, LANES), lambda b, hp: (b, off // LANES + hp))

    return pl.pallas_call(
        functools.partial(_attn_prompt_kernel, seq=seq),
        out_shape=jax.ShapeDtypeStruct((batch * seq, d_attn), F32),
        grid=(batch, n_pairs),
        in_specs=[
            col(q_off), col(k_off), col(v_off),
            pl.BlockSpec((None, N_PATTERNS, 2, stacked, 2 * BAND), lambda b, hp: (hp, 0, 0, 0, 0)),
        ],
        out_specs=pl.BlockSpec((seq, LANES), lambda b, hp: (b, hp)),
        scratch_shapes=[pltpu.VMEM((N_PATTERNS, seq, LANES), F32),
                        pltpu.VMEM((N_PATTERNS, seq, LANES), F32),
                        pltpu.VMEM((N_PATTERNS, seq, LANES), F32),
                        [pltpu.VMEM((seq, LANES), BF16) for _ in DILATED_PATTERNS],
                        [pltpu.VMEM((seq + d * BAND, LANES), BF16) for _, d in DILATED_PATTERNS],
                        [pltpu.VMEM((seq + d * BAND, LANES), BF16) for _, d in DILATED_PATTERNS],
                        [[pltpu.VMEM((ATTN_GROUP, stacked, 2 * BAND), F32),
                          pltpu.VMEM((ATTN_GROUP, stacked, 2 * BAND), BF16),
                          pltpu.VMEM((ATTN_GROUP, stacked, LANES), F32),
                          pltpu.VMEM((ATTN_GROUP, stacked, LANES), F32)]
                         for _ in range(2)]],
        compiler_params=_params("parallel", "parallel"),
        name="attn_prompt",
    )(z, z, z, bias)


def _pooled_window(ext_ref, row0, n_rows, cols, w, pos):
    cur = ext_ref[pl.ds(row0, n_rows), cols]
    acc = cur
    for i in range(1, w):
        acc = acc + ext_ref[pl.ds(row0 - i, n_rows), cols]
    return acc / jnp.minimum(w, pos + 1).astype(F32) - cur


POOL_ROWS = 64


def _prompt_mix(u_ref, halo_ref, gp_ref, ga_ref, a_ref, pw_ref, ps_ref, ext_ref, pooled_ref, mix_ref,
                *, tile, tm, d_pool):
    halo = halo_ref.shape[0]
    group = d_pool // len(POOL_WINDOWS)
    ext_ref[pl.ds(0, halo), :] = jnp.where(tile == 0, 0.0, halo_ref[...])
    ext_ref[pl.ds(halo, tm), :] = u_ref[...]
    for g, w in enumerate(POOL_WINDOWS):
        cols = slice(g * group, (g + 1) * group)
        for row in range(0, tm, POOL_ROWS):
            pos = tile * tm + row + lax.broadcasted_iota(jnp.int32, (POOL_ROWS, 1), 0)
            pooled = _pooled_window(ext_ref, halo + row, POOL_ROWS, cols, w, pos)
            pooled_ref[pl.ds(row, POOL_ROWS), cols] = pooled.astype(BF16)
        mixed = jnp.dot(pooled_ref[:, cols], pw_ref[g], preferred_element_type=F32) * ps_ref[:, cols]
        mix_ref[:, cols] = (mixed * _silu(gp_ref[:, cols])).astype(BF16)
    mix_ref[:, d_pool:] = (a_ref[...] * _silu(ga_ref[...])).astype(BF16)


def _out_proj_kernel(x_ref, u_ref, halo_ref, gp_ref, ga_ref, a_ref, pw_ref, ps_ref, wo_ref, gf_ref,
                     *out_and_scratch, tm, tiles_per_seq, d_pool, last):
    if last:
        o_ref, ext_ref, pooled_ref, mix_ref = out_and_scratch
    else:
        o_ref, h_ref, ext_ref, pooled_ref, mix_ref = out_and_scratch
    tile = pl.program_id(0) % tiles_per_seq
    _prompt_mix(u_ref, halo_ref, gp_ref, ga_ref, a_ref, pw_ref, ps_ref, ext_ref, pooled_ref, mix_ref,
                tile=tile, tm=tm, d_pool=d_pool)
    acc = x_ref[...] + jnp.dot(mix_ref[...], wo_ref[...], preferred_element_type=F32)
    normed = acc * lax.rsqrt(jnp.mean(acc * acc, axis=-1, keepdims=True) + RMS_EPS) * gf_ref[...]
    if last:
        o_ref[...] = normed
    else:
        o_ref[...] = acc
        h_ref[...] = normed.astype(BF16)


def _out_proj(x, z, a, pool_w_bf16, pool_scale, w_out_bf16, layer, next_g, *, seq, tm, d_pool, ga_off, last):
    m, d = x.shape
    halo = 2 * SUBLANES
    n_grp = len(POOL_WINDOWS)
    grp = d_pool // n_grp
    d_attn = a.shape[1]
    assert ga_off % d_attn == 0 and d_pool == d_attn and halo >= POOL_MAX - 1 and tm % POOL_ROWS == 0
    row_tile = pl.BlockSpec((tm, d), lambda i: (i, 0))
    out_shape = jax.ShapeDtypeStruct((m, d), F32)
    if last:
        out_specs = row_tile
    else:
        out_shape, out_specs = (out_shape, jax.ShapeDtypeStruct((m, d), BF16)), (row_tile, row_tile)
    return pl.pallas_call(
        functools.partial(_out_proj_kernel, tm=tm, tiles_per_seq=seq // tm, d_pool=d_pool, last=last),
        out_shape=out_shape,
        grid=(m // tm,),
        in_specs=[
            pl.BlockSpec((tm, d), lambda i: (i, 0)),
            pl.BlockSpec((tm, d_pool), lambda i: (i, 0)),
            pl.BlockSpec((halo, d_pool), lambda i: (jnp.maximum(i * (tm // halo) - 1, 0), 0)),
            pl.BlockSpec((tm, d_pool), lambda i: (i, 1)),
            pl.BlockSpec((tm, d_attn), lambda i: (i, ga_off // d_attn)),
            pl.BlockSpec((tm, d_attn), lambda i: (i, 0)),
            pl.BlockSpec((None, n_grp, grp, grp), lambda i: (layer, 0, 0, 0)),
            pl.BlockSpec((1, d_pool), lambda i: (0, 0)),
            pl.BlockSpec((d_pool + d_attn, d), lambda i: (0, 0), pipeline_mode=pl.Buffered(1)),
            pl.BlockSpec((1, d), lambda i: (0, 0)),
        ],
        out_specs=out_specs,
        scratch_shapes=[pltpu.VMEM((halo + tm, d_pool), F32),
                        pltpu.VMEM((tm, d_pool), BF16),
                        pltpu.VMEM((tm, d_pool + d_attn), BF16)],
        compiler_params=_params("parallel"),
        name="out_proj",
    )(x, z, z, z, z, a, pool_w_bf16, pool_scale.reshape(1, d_pool), w_out_bf16, next_g.reshape(1, d))


def _sample_bias_table(n_heads, l_win):
    slopes = _alibi_slopes(n_heads)
    dist = l_win - np.arange(l_win)
    table = np.empty((n_heads, SUBLANES, l_win), np.float32)
    for p, (window, d) in enumerate(DILATED_PATTERNS):
        valid = (dist % d == 0) & (dist <= window)
        table[:, p, :] = np.where(valid[None, :], -slopes[:, None] * dist[None, :].astype(np.float32), -np.inf)
    table[:, N_PATTERNS:, :] = table[:, :1, :]
    return jnp.asarray(table)


def _sample_mix_kernel(z_ref, st_ref, kt_ref, vt_ref, bias_ref, pw_ref, ps_ref,
                       mix_ref, st_out_ref, ext_ref, a_scr, *, d_pool, d_attn, pos0):
    n_heads = d_attn // HEAD_DIM
    u = z_ref[:, 0:d_pool]
    gp = z_ref[:, d_pool:2 * d_pool]
    ga = z_ref[:, 2 * d_pool + 3 * d_attn:2 * d_pool + 4 * d_attn]
    q_off, k_off, v_off = 2 * d_pool, 2 * d_pool + d_attn, 2 * d_pool + 2 * d_attn

    n_state = POOL_MAX - 1
    ext_ref[pl.ds(0, 1), :] = jnp.zeros((1, d_pool), F32)
    ext_ref[pl.ds(1, n_state), :] = st_ref[...]
    ext_ref[pl.ds(POOL_MAX, 1), :] = u
    pos = jnp.full((1, 1), pos0, jnp.int32)
    group = d_pool // len(POOL_WINDOWS)
    parts = []
    for g, w in enumerate(POOL_WINDOWS):
        cols = slice(g * group, (g + 1) * group)
        pooled = _pooled_window(ext_ref, POOL_MAX, 1, cols, w, pos)
        parts.append(jnp.dot(pooled.astype(BF16), pw_ref[g], preferred_element_type=F32))
    pm = jnp.concatenate(parts, axis=-1) * ps_ref[...]
    st_out_ref[pl.ds(0, n_state - 1), :] = st_ref[pl.ds(1, n_state - 1), :]
    st_out_ref[pl.ds(n_state - 1, 1), :] = u

    is_pattern = lax.broadcasted_iota(jnp.int32, (SUBLANES, 1), 0) < N_PATTERNS
    for h in range(n_heads):
        cols = slice(h * HEAD_DIM, (h + 1) * HEAD_DIM)

        def head_row(off, cols=cols):
            return z_ref[:, off:off + d_attn][:, cols]

        q8 = jnp.broadcast_to(head_row(q_off) * (HEAD_DIM ** -0.5), (SUBLANES, HEAD_DIM)).astype(BF16)
        k_new = head_row(k_off).astype(BF16).astype(F32)
        v_new = head_row(v_off).astype(BF16).astype(F32)
        s = jnp.dot(q8, kt_ref[h].astype(BF16), preferred_element_type=F32) + bias_ref[h]
        s_new = jnp.sum(q8.astype(F32) * k_new, axis=-1, keepdims=True)
        m = jnp.maximum(jnp.max(s, axis=-1, keepdims=True), s_new)
        p = jnp.exp(s - m)
        p_new = jnp.exp(s_new - m)
        den = jnp.sum(p, axis=-1, keepdims=True) + p_new
        o = lax.dot_general(p.astype(BF16), vt_ref[h].astype(BF16), NT_DIMS, preferred_element_type=F32)
        o = (o + p_new.astype(BF16).astype(F32) * v_new) / den
        lse = jnp.where(is_pattern, m + jnp.log(den), -jnp.inf)
        e = jnp.exp(lse - jnp.max(lse, axis=0, keepdims=True))
        a_scr[:, cols] = jnp.sum(e * o, axis=0, keepdims=True) / jnp.sum(e, axis=0, keepdims=True)

    mix_ref[:, :d_pool] = pm * _silu(gp)
    mix_ref[:, d_pool:] = a_scr[...] * _silu(ga)


def _sample_mix(z, state, kt_cache, vt_cache, layer, bias, pool_w_bf16, pool_scale, *, d_pool, d_attn, pos0):
    bd, n_state, _ = state.shape
    n_heads, _, l_win = kt_cache.shape[2:]
    n_in = z.shape[1]
    n_grp = len(POOL_WINDOWS)
    grp = d_pool // n_grp
    cache_spec = pl.BlockSpec((None, None, n_heads, HEAD_DIM, l_win), lambda b: (layer, b, 0, 0, 0))
    return pl.pallas_call(
        functools.partial(_sample_mix_kernel, d_pool=d_pool, d_attn=d_attn, pos0=pos0),
        out_shape=(jax.ShapeDtypeStruct((bd, 1, d_pool + d_attn), F32),
                   jax.ShapeDtypeStruct((bd, n_state, d_pool), F32)),
        grid=(bd,),
        in_specs=[
            pl.BlockSpec((None, 1, n_in), lambda b: (b, 0, 0)),
            pl.BlockSpec((None, n_state, d_pool), lambda b: (b, 0, 0)),
            cache_spec, cache_spec,
            pl.BlockSpec((n_heads, SUBLANES, l_win), lambda b: (0, 0, 0)),
            pl.BlockSpec((None, n_grp, grp, grp), lambda b: (layer, 0, 0, 0)),
            pl.BlockSpec((1, d_pool), lambda b: (0, 0)),
        ],
        out_specs=(pl.BlockSpec((None, 1, d_pool + d_attn), lambda b: (b, 0, 0)),
                   pl.BlockSpec((None, n_state, d_pool), lambda b: (b, 0, 0))),
        scratch_shapes=[pltpu.VMEM((POOL_MAX + SUBLANES, d_pool), F32),
                        pltpu.VMEM((1, d_attn), F32)],
        compiler_params=_params("parallel"),
        name="sample_mix",
    )(z.reshape(bd, 1, n_in), state, kt_cache, vt_cache, bias, pool_w_bf16,
      pool_scale.reshape(1, d_pool))


def _resid_matmul_cast_kernel(x_ref, mix_ref, w_ref, o_ref, wb_ref):
    wb_ref[...] = w_ref[...].astype(BF16)
    o_ref[...] = x_ref[...] + jnp.dot(mix_ref[...].astype(BF16), wb_ref[...], preferred_element_type=F32)


def _resid_matmul_cast(x, mix, w_f32, layer, *, tn):
    m, d = x.shape
    kdim = mix.shape[1]
    return pl.pallas_call(
        _resid_matmul_cast_kernel,
        out_shape=(jax.ShapeDtypeStruct((m, d), F32), jax.ShapeDtypeStruct((kdim, d), BF16)),
        grid=(d // tn,),
        in_specs=[pl.BlockSpec((m, tn), lambda j: (0, j)),
                  pl.BlockSpec((m, kdim), lambda j: (0, 0)),
                  pl.BlockSpec((None, kdim, tn), lambda j: (layer, 0, j))],
        out_specs=(pl.BlockSpec((m, tn), lambda j: (0, j)),
                   pl.BlockSpec((kdim, tn), lambda j: (0, j))),
        compiler_params=_params("parallel"),
        name="sample_out_proj",
    )(x, mix, w_f32)


def kernel(x_prompt, x_sample, state_pool, cache_k_win, cache_v_win, norm_g, w_in, pool_w, pool_scale,
           w_out, final_norm_g):
    batch, seq, d_model = x_prompt.shape
    bd, dec_seq, _ = x_sample.shape
    depth = norm_g.shape[0]
    d_pool = pool_scale.shape[1]
    d_attn = w_out.shape[1] - d_pool
    n_heads = d_attn // HEAD_DIM
    l_win = cache_k_win.shape[2]
    assert dec_seq == 1 and l_win == WIN_MAX and PAST_LEN + 1 >= POOL_MAX
    assert seq % (BAND * max(d for _, d in DILATED_PATTERNS)) == 0 and seq >= WIN_MAX
    q_off, k_off, v_off, ga_off = (2 * d_pool, 2 * d_pool + d_attn, 2 * d_pool + 2 * d_attn,
                                   2 * d_pool + 3 * d_attn)

    pool_w_b = pool_w.astype(BF16)
    bias_p = _prompt_bias_table(n_heads)
    bias_s = _sample_bias_table(n_heads, l_win)

    kt_cache = cache_k_win.transpose(0, 1, 3, 4, 2)
    vt_cache = cache_v_win.transpose(0, 1, 3, 4, 2)
    xs = x_sample.reshape(bd, d_model)
    pool_s, k_s, v_s, w_in_b, w_out_b = [], [], [], [], []
    for l in range(depth):
        z, wb = _in_proj_cast(xs, norm_g[l], w_in, l, tn=1024)
        w_in_b.append(wb)
        mix, st = _sample_mix(z, state_pool[l], kt_cache, vt_cache, l, bias_s, pool_w_b, pool_scale[l],
                              d_pool=d_pool, d_attn=d_attn, pos0=PAST_LEN)
        xs, wb = _resid_matmul_cast(xs, mix.reshape(bd, -1), w_out, l, tn=1024)
        w_out_b.append(wb)
        pool_s.append(st)
        k_s.append(z[:, k_off:k_off + d_attn].reshape(bd, 1, n_heads, HEAD_DIM))
        v_s.append(z[:, v_off:v_off + d_attn].reshape(bd, 1, n_heads, HEAD_DIM))
    y_sample = _rms_norm(xs, final_norm_g, tm=bd).reshape(bd, 1, d_model)

    keep = min(WIN_MAX, seq)
    xp = x_prompt.reshape(batch * seq, d_model)
    hp = _norm_cast(xp, norm_g[0], tm=512)
    pool_p, kt_p, vt_p = [], [], []
    for l in range(depth):
        z, kt, vt = _in_proj_window(hp, w_in_b[l], seq=seq, keep=keep, k_off=k_off, v_off=v_off,
                                    d_attn=d_attn, tm=1024, tn=2048)
        a = _attn_prompt(z, bias_p, batch=batch, seq=seq, q_off=q_off, k_off=k_off, v_off=v_off, d_attn=d_attn)
        last = l == depth - 1
        out = _out_proj(xp, z, a, pool_w_b, pool_scale[l], w_out_b[l], l, final_norm_g if last else norm_g[l + 1],
                        seq=seq, tm=512, d_pool=d_pool, ga_off=ga_off, last=last)
        if last:
            y_prompt = out.reshape(batch, seq, d_model)
        else:
            xp, hp = out
        pool_p.append(z.reshape(batch, seq, -1)[:, seq - (POOL_MAX - 1):, :d_pool])
        kt_p.append(kt)
        vt_p.append(vt)

    def from_slabs(slabs):
        return jnp.stack(slabs).reshape(depth, batch, n_heads, HEAD_DIM, keep).transpose(0, 1, 4, 2, 3)

    return (y_prompt, y_sample,
            jnp.stack(pool_p), from_slabs(kt_p), from_slabs(vt_p),
            jnp.stack(pool_s), jnp.stack(k_s), jnp.stack(v_s))
```

```python
import functools

import numpy as np
import jax
import jax.numpy as jnp
from jax import lax
from jax.experimental import pallas as pl
from jax.experimental.pallas import tpu as pltpu

HEAD_DIM = 64
LANES = 128
SUBLANES = 8
HEADS_PER_TILE = LANES // HEAD_DIM
POOL_WINDOWS = (2, 4, 8, 16)
POOL_MAX = max(POOL_WINDOWS)
DILATED_PATTERNS = ((128, 1), (512, 4), (2048, 16))
N_PATTERNS = len(DILATED_PATTERNS)
BAND = 128
WIN_MAX = max(w for w, _ in DILATED_PATTERNS)
PAST_LEN = 16384
RMS_EPS = 1e-6
VMEM_LIMIT_BYTES = 56 * 2**20

F32 = jnp.float32
BF16 = jnp.bfloat16
NT_DIMS = (((1,), (1,)), ((), ()))


def _alibi_slopes(n_heads):
    return (2.0 ** (-8.0 * np.arange(1, n_heads + 1) / n_heads)).astype(np.float32)


def _silu(x):
    return x * (1.0 / (1.0 + jnp.exp(-x)))


def _params(*sem):
    return pltpu.CompilerParams(dimension_semantics=sem, vmem_limit_bytes=VMEM_LIMIT_BYTES)


def _tile_plan(seq, d_model, d_attn):
    in_proj_rows = min(1024, seq)
    out_proj_rows = min(512, seq)
    in_proj_bytes = (2 * 4 + 2) * in_proj_rows * d_model + 2 * 2 * d_model * d_attn + 3 * 2 * 4 * in_proj_rows * d_attn
    out_proj_bytes = 2 * d_model * d_model + (2 * 2 + 4 * 2 // 2) * 4 * out_proj_rows * d_model
    assert max(in_proj_bytes, out_proj_bytes) <= VMEM_LIMIT_BYTES
    return dict(in_proj_rows=in_proj_rows, out_proj_rows=out_proj_rows, sample_cols=d_attn)


def _normed_bf16(x_ref, g_ref, h_ref):
    x = x_ref[...]
    y = x * lax.rsqrt(jnp.mean(x * x, axis=-1, keepdims=True) + RMS_EPS)
    h_ref[...] = (y * g_ref[...]).astype(BF16)


def _in_proj_cast_kernel(x_ref, g_ref, w_ref, z_ref, wb_ref, h_ref):
    @pl.when(pl.program_id(0) == 0)
    def _():
        _normed_bf16(x_ref, g_ref, h_ref)

    wb_ref[...] = w_ref[...].astype(BF16)
    z_ref[...] = jnp.dot(h_ref[...], wb_ref[...], preferred_element_type=F32)


def _in_proj_window_kernel(x_ref, g_ref, w_ref, *rest, tiles_per_seq, first_win_tile, k_tile, v_tile,
                           zero_fill):
    z_ref, kt_ref, vt_ref, h_ref = rest[-4:]
    j = pl.program_id(1)

    @pl.when(j == 0)
    def _():
        _normed_bf16(x_ref, g_ref, h_ref)

    z_ref[...] = jnp.dot(h_ref[...], w_ref[...], preferred_element_type=F32)
    in_window = pl.program_id(0) % tiles_per_seq >= first_win_tile

    @pl.when(jnp.logical_and(in_window, j == k_tile))
    def _():
        kt_ref[...] = z_ref[...].T

    @pl.when(jnp.logical_and(in_window, j == v_tile))
    def _():
        vt_ref[...] = z_ref[...].T

    if zero_fill:
        @pl.when(jnp.logical_not(in_window))
        def _():
            kt_ref[...] = jnp.zeros_like(kt_ref)
            vt_ref[...] = jnp.zeros_like(vt_ref)


def _in_proj_cast(x, g, w_f32, layer, *, tn):
    m, d = x.shape
    n = w_f32.shape[2]
    return pl.pallas_call(
        _in_proj_cast_kernel,
        out_shape=(jax.ShapeDtypeStruct((m, n), F32), jax.ShapeDtypeStruct((d, n), BF16)),
        grid=(n // tn,),
        in_specs=[
            pl.BlockSpec((m, d), lambda j: (0, 0)),
            pl.BlockSpec((None, 1, d), lambda j: (layer, 0, 0)),
            pl.BlockSpec((None, d, tn), lambda j: (layer, 0, j)),
        ],
        out_specs=(pl.BlockSpec((m, tn), lambda j: (0, j)),
                   pl.BlockSpec((d, tn), lambda j: (0, j))),
        scratch_shapes=[pltpu.VMEM((m, d), BF16)],
        compiler_params=_params("arbitrary"),
        name="in_proj_sample",
    )(x, g, w_f32)


def _in_proj_window(x, g, w_bf16, layer, depth, windows, *, seq, keep, k_off, v_off, d_attn, tm):
    m, d = x.shape
    n = w_bf16.shape[1]
    tn = d_attn
    assert k_off % tn == 0 and v_off % tn == 0 and (seq - keep) % tm == 0 and seq % tm == 0
    batch = m // seq
    tiles_per_seq = seq // tm
    first_win_tile = (seq - keep) // tm
    win_tiles = tiles_per_seq - first_win_tile
    n_col = n // tn
    create = windows is None
    zero_fill = create and depth > 1
    n_slots = batch * first_win_tile * n_col
    n_zero = (depth - 1) * batch * win_tiles
    assert (layer == 0) == create and (not zero_fill or n_slots >= n_zero)

    def win_map(i, j):
        b, t = i // tiles_per_seq, i % tiles_per_seq
        own = (layer, b, 0, jnp.maximum(t - first_win_tile, 0))
        if not zero_fill:
            return own
        zero_block = ((b * first_win_tile + t) * n_col + j) * n_zero // n_slots
        other = (1 + zero_block // (batch * win_tiles), (zero_block // win_tiles) % batch, 0,
                 zero_block % win_tiles)
        return tuple(jnp.where(t >= first_win_tile, o, z) for o, z in zip(own, other))

    win = jax.ShapeDtypeStruct((depth, batch, d_attn, keep), F32)
    win_spec = pl.BlockSpec((None, None, d_attn, tm), win_map)
    extra_specs = [] if create else [pl.BlockSpec(memory_space=pl.ANY)] * 2
    return pl.pallas_call(
        functools.partial(_in_proj_window_kernel, tiles_per_seq=tiles_per_seq, first_win_tile=first_win_tile,
                          k_tile=k_off // tn, v_tile=v_off // tn, zero_fill=zero_fill),
        out_shape=(jax.ShapeDtypeStruct((m, n), F32), win, win),
        grid=(m // tm, n_col),
        in_specs=[
            pl.BlockSpec((tm, d), lambda i, j: (i, 0)),
            pl.BlockSpec((None, 1, d), lambda i, j: (layer, 0, 0)),
            pl.BlockSpec((d, tn), lambda i, j: (0, j)),
            *extra_specs,
        ],
        out_specs=(pl.BlockSpec((tm, tn), lambda i, j: (i, j)), win_spec, win_spec),
        scratch_shapes=[pltpu.VMEM((tm, d), BF16)],
        input_output_aliases={} if create else {3: 1, 4: 2},
        compiler_params=_params("arbitrary", "arbitrary"),
        name="in_proj_prompt",
    )(x, g, w_bf16, *(() if create else windows))


def _rms_norm_kernel(x_ref, g_ref, o_ref):
    x = x_ref[...]
    y = x * lax.rsqrt(jnp.mean(x * x, axis=-1, keepdims=True) + RMS_EPS)
    o_ref[...] = y * g_ref[...]


def _rms_norm(x, g, *, tm):
    m, d = x.shape
    return pl.pallas_call(
        _rms_norm_kernel,
        out_shape=jax.ShapeDtypeStruct((m, d), F32),
        grid=(m // tm,),
        in_specs=[pl.BlockSpec((tm, d), lambda i: (i, 0)),
                  pl.BlockSpec((1, d), lambda i: (0, 0))],
        out_specs=pl.BlockSpec((tm, d), lambda i: (i, 0)),
        compiler_params=_params("parallel"),
        name="final_norm",
    )(x, g.reshape(1, d))


def _prompt_bias_table(n_heads):
    slopes = _alibi_slopes(n_heads)
    qi = np.arange(BAND)[:, None] + BAND
    kj = np.arange(2 * BAND)[None, :]
    dist = qi - kj
    valid = (dist >= 0) & (dist <= BAND)
    valid_first = valid & (kj >= BAND)
    table = np.empty((n_heads, N_PATTERNS, 2, BAND, 2 * BAND), np.float32)
    for h in range(n_heads):
        for p, (_, d) in enumerate(DILATED_PATTERNS):
            bias = -slopes[h] * (dist * d).astype(np.float32)
            table[h, p, 0] = np.where(valid, bias, -np.inf)
            table[h, p, 1] = np.where(valid_first, bias, -np.inf)
    table = table.reshape(n_heads // HEADS_PER_TILE, HEADS_PER_TILE, N_PATTERNS, 2, BAND, 2 * BAND)
    table = table.transpose(0, 2, 3, 1, 4, 5)
    return jnp.asarray(table.reshape(n_heads // HEADS_PER_TILE, N_PATTERNS, 2, HEADS_PER_TILE * BAND, 2 * BAND))


ATTN_GROUP = 4
SOFTMAX_ROWS = 32


PREP_ROWS = 256


def _attn_prompt_kernel(q_ref, k_ref, v_ref, bias_ref, a_ref,
                        o_scr, m_scr, d_scr, qd_scrs, kd_scrs, vd_scrs,
                        group_bufs, *, seq):
    lane = lax.broadcasted_iota(jnp.int32, (BAND, LANES), 1)
    head0 = lane < HEAD_DIM
    stacked = HEADS_PER_TILE * BAND
    scale = HEAD_DIM ** -0.5
    srcs = (q_ref, k_ref, v_ref)
    assert DILATED_PATTERNS[1][1] ** 2 == DILATED_PATTERNS[2][1] and DILATED_PATTERNS[0][1] == 1
    d_mid, d_big = DILATED_PATTERNS[1][1], DILATED_PATTERNS[2][1]

    def put(ti, pi, sub, row, val):
        n_sub = seq // DILATED_PATTERNS[pi][1]
        if ti == 0:
            qd_scrs[pi][pl.ds(sub * n_sub + row, PREP_ROWS), :] = (val * scale).astype(BF16)
        else:
            dst = (kd_scrs, vd_scrs)[ti - 1][pi]
            dst[pl.ds(sub * (BAND + n_sub) + BAND + row, PREP_ROWS), :] = val.astype(BF16)

    for pi, (_, d) in enumerate(DILATED_PATTERNS):
        n_sub = seq // d
        for dst in (kd_scrs[pi], vd_scrs[pi]):
            for r in range(d):
                dst[pl.ds(r * (BAND + n_sub), BAND), :] = jnp.zeros((BAND, LANES), BF16)

    def prep_dense(c, carry):
        row = pl.multiple_of(c * PREP_ROWS, PREP_ROWS)
        for ti in range(3):
            put(ti, 0, 0, row, srcs[ti][pl.ds(row, PREP_ROWS), :])
        return carry

    lax.fori_loop(0, seq // PREP_ROWS, prep_dense, 0)

    n_mid = seq // d_mid

    def prep_mid(c, carry):
        row = pl.multiple_of(c * PREP_ROWS, PREP_ROWS)
        for ti in range(3):
            for r in range(d_mid):
                val = srcs[ti][pl.ds(r + d_mid * row, PREP_ROWS, stride=d_mid), :]
                o_scr[ti, pl.ds(r * n_mid + row, PREP_ROWS), :] = val
                put(ti, 1, r, row, val)
        return carry

    lax.fori_loop(0, n_mid // PREP_ROWS, prep_mid, 0)

    n_big = seq // d_big
    for c in range(n_big // PREP_ROWS):
        for ti in range(3):
            for r in range(d_mid):
                for a in range(d_mid):
                    val = o_scr[ti, pl.ds(r * n_mid + a + d_mid * c * PREP_ROWS, PREP_ROWS, stride=d_mid), :]
                    put(ti, 2, r + d_mid * a, c * PREP_ROWS, val)

    def aligned(row):
        return row if isinstance(row, int) else pl.multiple_of(row, BAND)

    stages = []
    for pi, (_, d) in enumerate(DILATED_PATTERNS):
        nblk = seq // d // BAND
        n_sub = seq // d
        qd, kd, vd = qd_scrs[pi], kd_scrs[pi], vd_scrs[pi]

        def units_of(grp, d=d, nblk=nblk, n_sub=n_sub):
            units = []
            for g in range(ATTN_GROUP):
                idx = grp * ATTN_GROUP + g
                r = idx // nblk
                blk = idx % nblk
                q_row = aligned(r * n_sub + blk * BAND)
                k_row = aligned(r * (BAND + n_sub) + blk * BAND)
                if d == d_big:
                    out_rows = pl.ds((r % d_mid) * n_mid + r // d_mid + d_mid * BAND * blk, BAND, stride=d_mid)
                else:
                    out_rows = pl.ds(q_row, BAND)
                first = int(blk == 0) if isinstance(blk, int) else jnp.asarray(blk == 0, jnp.int32)
                units.append((q_row, k_row, out_rows, first))
            return units

        def scores(units, bufs, pi=pi, qd=qd, kd=kd):
            s_buf, _, mg_buf, _ = bufs
            for g, (q_row, k_row, _, first) in enumerate(units):
                q = qd[pl.ds(q_row, BAND), :]
                zero = jnp.zeros_like(q)
                q2 = jnp.concatenate([jnp.where(head0, q, zero), jnp.where(head0, zero, q)], axis=0)
                s = lax.dot_general(q2, kd[pl.ds(k_row, 2 * BAND), :], NT_DIMS,
                                    preferred_element_type=F32) + bias_ref[pi, first]
                s_buf[g] = s
                mg_buf[g] = jnp.broadcast_to(jnp.max(s, axis=-1, keepdims=True), (stacked, LANES))

        def softmax_values(units, bufs, pi=pi, vd=vd):
            s_buf, p_buf, mg_buf, dg_buf = bufs
            for g in range(ATTN_GROUP):
                for c in range(stacked // SOFTMAX_ROWS):
                    sl = pl.ds(c * SOFTMAX_ROWS, SOFTMAX_ROWS)
                    m = mg_buf[g, sl, :]
                    p = jnp.exp(s_buf[g, sl, :] - jnp.concatenate([m, m], axis=-1))
                    p_buf[g, sl, :] = p.astype(BF16)
                    dg_buf[g, sl, :] = jnp.broadcast_to(jnp.sum(p, axis=-1, keepdims=True), (SOFTMAX_ROWS, LANES))
            for g, (_, _, out_rows, _) in enumerate(units):
                m_scr[pi, out_rows, :] = jnp.where(head0, mg_buf[g, pl.ds(0, BAND), :], mg_buf[g, pl.ds(BAND, BAND), :])
                d_scr[pi, out_rows, :] = jnp.where(head0, dg_buf[g, pl.ds(0, BAND), :], dg_buf[g, pl.ds(BAND, BAND), :])
            for g, (_, k_row, out_rows, _) in enumerate(units):
                o2 = jnp.dot(p_buf[g], vd[pl.ds(k_row, 2 * BAND), :], preferred_element_type=F32)
                o_scr[pi, out_rows, :] = jnp.where(head0, o2[:BAND], o2[BAND:])

        stages.append((units_of, scores, softmax_values, d * nblk // ATTN_GROUP))

    def pair(j, units_of, scores, softmax_values, score_ahead):
        even, odd = units_of(2 * j), units_of(2 * j + 1)
        scores(odd, group_bufs[1])
        softmax_values(even, group_bufs[0])
        score_ahead()
        softmax_values(odd, group_bufs[1])

    stages[0][1](stages[0][0](0), group_bufs[0])
    for pi, (units_of, scores, softmax_values, n_groups) in enumerate(stages):
        def body(j, carry, units_of=units_of, scores=scores, softmax_values=softmax_values):
            pair(j, units_of, scores, softmax_values, lambda: scores(units_of(2 * j + 2), group_bufs[0]))
            return carry

        lax.fori_loop(0, n_groups // 2 - 1, body, 0)
        if pi + 1 < len(stages):
            next_units_of, next_scores = stages[pi + 1][:2]
            score_ahead = functools.partial(next_scores, next_units_of(0), group_bufs[0])
        else:
            score_ahead = lambda: None
        pair(n_groups // 2 - 1, units_of, scores, softmax_values, score_ahead)

    chunk = 2 * BAND

    def combine(c, carry):
        start = pl.multiple_of(c * chunk, chunk)
        sl = pl.ds(start, chunk)
        natural = pl.ds(start // n_mid + d_mid * (start % n_mid), chunk, stride=d_mid)
        ms = [m_scr[0, natural, :], m_scr[1, sl, :], m_scr[2, sl, :]]
        m = jnp.maximum(jnp.maximum(ms[0], ms[1]), ms[2])
        es = [jnp.exp(mi - m) for mi in ms]
        num = es[0] * o_scr[0, natural, :] + es[1] * o_scr[1, sl, :] + es[2] * o_scr[2, sl, :]
        den = es[0] * d_scr[0, natural, :] + es[1] * d_scr[1, sl, :] + es[2] * d_scr[2, sl, :]
        a_ref[natural, :] = num / den
        return carry

    lax.fori_loop(0, seq // chunk, combine, 0)


def _attn_prompt(z, bias, *, batch, seq, q_off, k_off, v_off, d_attn):
    n_pairs = d_attn // LANES
    stacked = HEADS_PER_TILE * BAND

    def col(off):
        return pl.BlockSpec((seq, LANES), lambda b, hp: (b, off // LANES + hp))

    return pl.pallas_call(
        functools.partial(_attn_prompt_kernel, seq=seq),
        out_shape=jax.ShapeDtypeStruct((batch * seq, d_attn), F32),
        grid=(batch, n_pairs),
        in_specs=[
            col(q_off), col(k_off), col(v_off),
            pl.BlockSpec((None, N_PATTERNS, 2, stacked, 2 * BAND), lambda b, hp: (hp, 0, 0, 0, 0)),
        ],
        out_specs=pl.BlockSpec((seq, LANES), lambda b, hp: (b, hp)),
        scratch_shapes=[pltpu.VMEM((N_PATTERNS, seq, LANES), F32),
                        pltpu.VMEM((N_PATTERNS, seq, LANES), F32),
                        pltpu.VMEM((N_PATTERNS, seq, LANES), F32),
                        [pltpu.VMEM((seq, LANES), BF16) for _ in DILATED_PATTERNS],
                        [pltpu.VMEM((seq + d * BAND, LANES), BF16) for _, d in DILATED_PATTERNS],
                        [pltpu.VMEM((seq + d * BAND, LANES), BF16) for _, d in DILATED_PATTERNS],
                        [[pltpu.VMEM((ATTN_GROUP, stacked, 2 * BAND), F32),
                          pltpu.VMEM((ATTN_GROUP, stacked, 2 * BAND), BF16),
                          pltpu.VMEM((ATTN_GROUP, stacked, LANES), F32),
                          pltpu.VMEM((ATTN_GROUP, stacked, LANES), F32)]
                         for _ in range(2)]],
        compiler_params=_params("parallel", "parallel"),
        name="attn_prompt",
    )(z, z, z, bias)


def _pooled_window(ext_ref, row0, n_rows, cols, w, pos):
    cur = ext_ref[pl.ds(row0, n_rows), cols]
    acc = cur
    for i in range(1, w):
        acc = acc + ext_ref[pl.ds(row0 - i, n_rows), cols]
    return acc / jnp.minimum(w, pos + 1).astype(F32) - cur


POOL_ROWS = 64


def _prompt_mix(u_ref, halo_ref, gp_ref, ga_ref, a_ref, pw_ref, ps_ref, ext_ref, level_refs, pooled_ref, mix_ref,
                *, tile, tm, d_pool):
    halo = halo_ref.shape[0]
    n_grp = len(POOL_WINDOWS)
    group = d_pool // n_grp
    assert all(w == 2 << g for g, w in enumerate(POOL_WINDOWS)) and halo >= (n_grp) * SUBLANES
    ext_ref[pl.ds(0, halo), :] = jnp.where(tile == 0, 0.0, halo_ref[...])
    ext_ref[pl.ds(halo, tm), :] = u_ref[...]
    below, below_cols = ext_ref, slice(0, d_pool)
    for g, w in enumerate(POOL_WINDOWS):
        shift = w // 2
        level = level_refs[g] if g + 1 < n_grp else None
        own_cols = slice(g * group, (g + 1) * group)

        def window_sum(row, n, below=below, below_cols=below_cols, shift=shift):
            return below[pl.ds(row, n), below_cols] + below[pl.ds(row - shift, n), below_cols]

        first = (g + 1) * SUBLANES
        if level is not None and first < halo:
            level[pl.ds(first, halo - first), :] = window_sum(first, halo - first)
        for row in range(0, tm, POOL_ROWS):
            s = window_sum(halo + row, POOL_ROWS)
            if level is not None:
                level[pl.ds(halo + row, POOL_ROWS), :] = s
            pos = tile * tm + row + lax.broadcasted_iota(jnp.int32, (POOL_ROWS, 1), 0)
            cnt = jnp.minimum(w, pos + 1).astype(F32)
            pooled = s[:, :group] / cnt - ext_ref[pl.ds(halo + row, POOL_ROWS), own_cols]
            pooled_ref[pl.ds(row, POOL_ROWS), own_cols] = pooled.astype(BF16)
        mixed = jnp.dot(pooled_ref[:, own_cols], pw_ref[g], preferred_element_type=F32) * ps_ref[:, own_cols]
        mix_ref[:, own_cols] = (mixed * _silu(gp_ref[:, own_cols])).astype(BF16)
        if level is not None:
            below, below_cols = level, slice(group, d_pool - g * group)
    mix_ref[:, d_pool:] = (a_ref[...] * _silu(ga_ref[...])).astype(BF16)


def _out_proj_kernel(x_ref, u_ref, halo_ref, gp_ref, ga_ref, a_ref, pw_ref, ps_ref, wo_ref, gf_ref,
                     o_ref, ext_ref, level_refs, pooled_ref, mix_ref, *, tm, tiles_per_seq, d_pool, final_norm):
    tile = pl.program_id(0) % tiles_per_seq
    _prompt_mix(u_ref, halo_ref, gp_ref, ga_ref, a_ref, pw_ref, ps_ref, ext_ref, level_refs, pooled_ref, mix_ref,
                tile=tile, tm=tm, d_pool=d_pool)
    acc = x_ref[...] + jnp.dot(mix_ref[...], wo_ref[...], preferred_element_type=F32)
    if final_norm:
        acc = acc * lax.rsqrt(jnp.mean(acc * acc, axis=-1, keepdims=True) + RMS_EPS) * gf_ref[...]
    o_ref[...] = acc


def _out_proj(x, z, a, pool_w_bf16, pool_scale, w_out_bf16, layer, final_g, *, seq, tm, d_pool, ga_off,
              final_norm):
    m, d = x.shape
    n_grp = len(POOL_WINDOWS)
    halo = n_grp * SUBLANES
    grp = d_pool // n_grp
    d_attn = a.shape[1]
    assert ga_off % d_attn == 0 and d_pool == d_attn and halo >= POOL_MAX - 1 and tm % POOL_ROWS == 0
    return pl.pallas_call(
        functools.partial(_out_proj_kernel, tm=tm, tiles_per_seq=seq // tm, d_pool=d_pool,
                          final_norm=final_norm),
        out_shape=jax.ShapeDtypeStruct((m, d), F32),
        grid=(m // tm,),
        in_specs=[
            pl.BlockSpec((tm, d), lambda i: (i, 0)),
            pl.BlockSpec((tm, d_pool), lambda i: (i, 0)),
            pl.BlockSpec((halo, d_pool), lambda i: (jnp.maximum(i * (tm // halo) - 1, 0), 0)),
            pl.BlockSpec((tm, d_pool), lambda i: (i, 1)),
            pl.BlockSpec((tm, d_attn), lambda i: (i, ga_off // d_attn)),
            pl.BlockSpec((tm, d_attn), lambda i: (i, 0)),
            pl.BlockSpec((None, n_grp, grp, grp), lambda i: (layer, 0, 0, 0)),
            pl.BlockSpec((None, 1, d_pool), lambda i: (layer, 0, 0)),
            pl.BlockSpec((d_pool + d_attn, d), lambda i: (0, 0), pipeline_mode=pl.Buffered(1)),
            pl.BlockSpec((1, d), lambda i: (0, 0)),
        ],
        out_specs=pl.BlockSpec((tm, d), lambda i: (i, 0)),
        scratch_shapes=[pltpu.VMEM((halo + tm, d_pool), F32),
                        [pltpu.VMEM((halo + tm, d_pool - g * grp), F32) for g in range(n_grp - 1)],
                        pltpu.VMEM((tm, d_pool), BF16),
                        pltpu.VMEM((tm, d_pool + d_attn), BF16)],
        compiler_params=_params("parallel"),
        name="out_proj",
    )(x, z, z, z, z, a, pool_w_bf16, pool_scale, w_out_bf16, final_g.reshape(1, d))


def _sample_bias_table(n_heads, l_win):
    slopes = _alibi_slopes(n_heads)
    dist = l_win - np.arange(l_win)
    table = np.empty((n_heads, SUBLANES, l_win), np.float32)
    for p, (window, d) in enumerate(DILATED_PATTERNS):
        valid = (dist % d == 0) & (dist <= window)
        table[:, p, :] = np.where(valid[None, :], -slopes[:, None] * dist[None, :].astype(np.float32), -np.inf)
    table[:, N_PATTERNS:, :] = table[:, :1, :]
    return jnp.asarray(table)


def _sample_mix_kernel(z_ref, st_ref, kt_ref, vt_ref, bias_ref, pw_ref, ps_ref,
                       mix_ref, st_out_ref, ext_ref, a_scr, *, d_pool, d_attn, pos0):
    n_heads = d_attn // HEAD_DIM
    u = z_ref[:, 0:d_pool]
    gp = z_ref[:, d_pool:2 * d_pool]
    ga = z_ref[:, 2 * d_pool + 3 * d_attn:2 * d_pool + 4 * d_attn]
    q_off, k_off, v_off = 2 * d_pool, 2 * d_pool + d_attn, 2 * d_pool + 2 * d_attn

    n_state = POOL_MAX - 1
    ext_ref[pl.ds(0, 1), :] = jnp.zeros((1, d_pool), F32)
    ext_ref[pl.ds(1, n_state), :] = st_ref[...]
    ext_ref[pl.ds(POOL_MAX, 1), :] = u
    pos = jnp.full((1, 1), pos0, jnp.int32)
    group = d_pool // len(POOL_WINDOWS)
    parts = []
    for g, w in enumerate(POOL_WINDOWS):
        cols = slice(g * group, (g + 1) * group)
        pooled = _pooled_window(ext_ref, POOL_MAX, 1, cols, w, pos)
        parts.append(jnp.dot(pooled.astype(BF16), pw_ref[g], preferred_element_type=F32))
    pm = jnp.concatenate(parts, axis=-1) * ps_ref[...]
    st_out_ref[pl.ds(0, n_state - 1), :] = st_ref[pl.ds(1, n_state - 1), :]
    st_out_ref[pl.ds(n_state - 1, 1), :] = u

    is_pattern = lax.broadcasted_iota(jnp.int32, (SUBLANES, 1), 0) < N_PATTERNS
    for h in range(n_heads):
        cols = slice(h * HEAD_DIM, (h + 1) * HEAD_DIM)

        def head_row(off, cols=cols):
            return z_ref[:, off:off + d_attn][:, cols]

        q8 = jnp.broadcast_to(head_row(q_off) * (HEAD_DIM ** -0.5), (SUBLANES, HEAD_DIM)).astype(BF16)
        k_new = head_row(k_off).astype(BF16).astype(F32)
        v_new = head_row(v_off).astype(BF16).astype(F32)
        s = jnp.dot(q8, kt_ref[h].astype(BF16), preferred_element_type=F32) + bias_ref[h]
        s_new = jnp.sum(q8.astype(F32) * k_new, axis=-1, keepdims=True)
        m = jnp.maximum(jnp.max(s, axis=-1, keepdims=True), s_new)
        p = jnp.exp(s - m)
        p_new = jnp.exp(s_new - m)
        den = jnp.sum(p, axis=-1, keepdims=True) + p_new
        o = lax.dot_general(p.astype(BF16), vt_ref[h].astype(BF16), NT_DIMS, preferred_element_type=F32)
        o = (o + p_new.astype(BF16).astype(F32) * v_new) / den
        lse = jnp.where(is_pattern, m + jnp.log(den), -jnp.inf)
        e = jnp.exp(lse - jnp.max(lse, axis=0, keepdims=True))
        a_scr[:, cols] = jnp.sum(e * o, axis=0, keepdims=True) / jnp.sum(e, axis=0, keepdims=True)

    mix_ref[:, :d_pool] = pm * _silu(gp)
    mix_ref[:, d_pool:] = a_scr[...] * _silu(ga)


def _sample_mix(z, state, kt_cache, vt_cache, layer, bias, pool_w_bf16, pool_scale, *, d_pool, d_attn, pos0):
    bd, n_state, _ = state.shape
    n_heads, _, l_win = kt_cache.shape[2:]
    n_in = z.shape[1]
    n_grp = len(POOL_WINDOWS)
    grp = d_pool // n_grp
    cache_spec = pl.BlockSpec((None, None, n_heads, HEAD_DIM, l_win), lambda b: (layer, b, 0, 0, 0))
    return pl.pallas_call(
        functools.partial(_sample_mix_kernel, d_pool=d_pool, d_attn=d_attn, pos0=pos0),
        out_shape=(jax.ShapeDtypeStruct((bd, 1, d_pool + d_attn), F32),
                   jax.ShapeDtypeStruct((bd, n_state, d_pool), F32)),
        grid=(bd,),
        in_specs=[
            pl.BlockSpec((None, 1, n_in), lambda b: (b, 0, 0)),
            pl.BlockSpec((None, n_state, d_pool), lambda b: (b, 0, 0)),
            cache_spec, cache_spec,
            pl.BlockSpec((n_heads, SUBLANES, l_win), lambda b: (0, 0, 0)),
            pl.BlockSpec((None, n_grp, grp, grp), lambda b: (layer, 0, 0, 0)),
            pl.BlockSpec((None, 1, d_pool), lambda b: (layer, 0, 0)),
        ],
        out_specs=(pl.BlockSpec((None, 1, d_pool + d_attn), lambda b: (b, 0, 0)),
                   pl.BlockSpec((None, n_state, d_pool), lambda b: (b, 0, 0))),
        scratch_shapes=[pltpu.VMEM((POOL_MAX + SUBLANES, d_pool), F32),
                        pltpu.VMEM((1, d_attn), F32)],
        compiler_params=_params("parallel"),
        name="sample_mix",
    )(z.reshape(bd, 1, n_in), state, kt_cache, vt_cache, bias, pool_w_bf16, pool_scale)


def _resid_matmul_cast_kernel(x_ref, mix_ref, w_ref, o_ref, wb_ref):
    wb_ref[...] = w_ref[...].astype(BF16)
    o_ref[...] = x_ref[...] + jnp.dot(mix_ref[...].astype(BF16), wb_ref[...], preferred_element_type=F32)


def _resid_matmul_cast(x, mix, w_f32, layer, *, tn):
    m, d = x.shape
    kdim = mix.shape[1]
    return pl.pallas_call(
        _resid_matmul_cast_kernel,
        out_shape=(jax.ShapeDtypeStruct((m, d), F32), jax.ShapeDtypeStruct((kdim, d), BF16)),
        grid=(d // tn,),
        in_specs=[pl.BlockSpec((m, tn), lambda j: (0, j)),
                  pl.BlockSpec((m, kdim), lambda j: (0, 0)),
                  pl.BlockSpec((None, kdim, tn), lambda j: (layer, 0, j))],
        out_specs=(pl.BlockSpec((m, tn), lambda j: (0, j)),
                   pl.BlockSpec((kdim, tn), lambda j: (0, j))),
        compiler_params=_params("parallel"),
        name="sample_out_proj",
    )(x, mix, w_f32)


def kernel(x_prompt, x_sample, state_pool, cache_k_win, cache_v_win, norm_g, w_in, pool_w, pool_scale,
           w_out, final_norm_g):
    batch, seq, d_model = x_prompt.shape
    bd, dec_seq, _ = x_sample.shape
    depth = norm_g.shape[0]
    d_pool = pool_scale.shape[1]
    d_attn = w_out.shape[1] - d_pool
    n_heads = d_attn // HEAD_DIM
    l_win = cache_k_win.shape[2]
    assert dec_seq == 1 and l_win == WIN_MAX and PAST_LEN + 1 >= POOL_MAX
    assert seq % (BAND * max(d for _, d in DILATED_PATTERNS)) == 0 and seq >= WIN_MAX
    q_off, k_off, v_off, ga_off = (2 * d_pool, 2 * d_pool + d_attn, 2 * d_pool + 2 * d_attn,
                                   2 * d_pool + 3 * d_attn)

    tiles = _tile_plan(seq, d_model, d_attn)
    pool_w_b = pool_w.astype(BF16)
    gains = norm_g.reshape(depth, 1, d_model)
    scales = pool_scale.reshape(depth, 1, d_pool)
    bias_p = _prompt_bias_table(n_heads)
    bias_s = _sample_bias_table(n_heads, l_win)

    kt_cache = cache_k_win.transpose(0, 1, 3, 4, 2)
    vt_cache = cache_v_win.transpose(0, 1, 3, 4, 2)
    xs = x_sample.reshape(bd, d_model)
    pool_s, k_s, v_s, w_in_b, w_out_b = [], [], [], [], []
    for l in range(depth):
        z, wb = _in_proj_cast(xs, gains, w_in, l, tn=tiles["sample_cols"])
        w_in_b.append(wb)
        mix, st = _sample_mix(z, state_pool[l], kt_cache, vt_cache, l, bias_s, pool_w_b, scales,
                              d_pool=d_pool, d_attn=d_attn, pos0=PAST_LEN)
        xs, wb = _resid_matmul_cast(xs, mix.reshape(bd, -1), w_out, l, tn=tiles["sample_cols"])
        w_out_b.append(wb)
        pool_s.append(st)
        k_s.append(z[:, k_off:k_off + d_attn].reshape(bd, 1, n_heads, HEAD_DIM))
        v_s.append(z[:, v_off:v_off + d_attn].reshape(bd, 1, n_heads, HEAD_DIM))
    y_sample = _rms_norm(xs, final_norm_g, tm=bd).reshape(bd, 1, d_model)

    keep = min(WIN_MAX, seq)
    xp = x_prompt.reshape(batch * seq, d_model)
    pool_p, windows = [], None
    for l in range(depth):
        z, *windows = _in_proj_window(xp, gains, w_in_b[l], l, depth, windows, seq=seq, keep=keep,
                                      k_off=k_off, v_off=v_off, d_attn=d_attn, tm=tiles["in_proj_rows"])
        a = _attn_prompt(z, bias_p, batch=batch, seq=seq, q_off=q_off, k_off=k_off, v_off=v_off, d_attn=d_attn)
        xp = _out_proj(xp, z, a, pool_w_b, scales, w_out_b[l], l, final_norm_g, seq=seq, tm=tiles["out_proj_rows"],
                       d_pool=d_pool, ga_off=ga_off, final_norm=(l == depth - 1))
        pool_p.append(z.reshape(batch, seq, -1)[:, seq - (POOL_MAX - 1):, :d_pool])
    y_prompt = xp.reshape(batch, seq, d_model)

    def from_slabs(slabs):
        return slabs.reshape(depth, batch, n_heads, HEAD_DIM, keep).transpose(0, 1, 4, 2, 3)

    return (y_prompt, y_sample,
            jnp.stack(pool_p), from_slabs(windows[0]), from_slabs(windows[1]),
            jnp.stack(pool_s), jnp.stack(k_s), jnp.stack(v_s))
```

```python
import functools

import numpy as np
import jax
import jax.numpy as jnp
from jax import lax
from jax.experimental import pallas as pl
from jax.experimental.pallas import tpu as pltpu

HEAD_DIM = 64
LANES = 128
SUBLANES = 8
HEADS_PER_TILE = LANES // HEAD_DIM
POOL_WINDOWS = (2, 4, 8, 16)
POOL_MAX = max(POOL_WINDOWS)
DILATED_PATTERNS = ((128, 1), (512, 4), (2048, 16))
N_PATTERNS = len(DILATED_PATTERNS)
BAND = 128
WIN_MAX = max(w for w, _ in DILATED_PATTERNS)
PAST_LEN = 16384
RMS_EPS = 1e-6
VMEM_LIMIT_BYTES = 56 * 2**20

F32 = jnp.float32
BF16 = jnp.bfloat16
NT_DIMS = (((1,), (1,)), ((), ()))


def _alibi_slopes(n_heads):
    return (2.0 ** (-8.0 * np.arange(1, n_heads + 1) / n_heads)).astype(np.float32)


def _silu(x):
    return x * (1.0 / (1.0 + jnp.exp(-x)))


def _params(*sem):
    return pltpu.CompilerParams(dimension_semantics=sem, vmem_limit_bytes=VMEM_LIMIT_BYTES)


def _tile_plan(seq, d_model, d_attn):
    in_proj_rows = min(1024, seq)
    out_proj_rows = min(512, seq)
    in_proj_bytes = (2 * 4 + 2) * in_proj_rows * d_model + 2 * 2 * d_model * d_attn + 3 * 2 * 4 * in_proj_rows * d_attn
    out_proj_bytes = 2 * d_model * d_model + (2 * 2 + 4 * 2 // 2) * 4 * out_proj_rows * d_model
    assert max(in_proj_bytes, out_proj_bytes) <= VMEM_LIMIT_BYTES
    return dict(in_proj_rows=in_proj_rows, out_proj_rows=out_proj_rows, sample_cols=d_attn)


NORM_ROWS = 256


def _normed_bf16(x_ref, g_ref, h_ref):
    x = x_ref[...]
    y = x * lax.rsqrt(jnp.mean(x * x, axis=-1, keepdims=True) + RMS_EPS)
    h_ref[...] = (y * g_ref[...]).astype(BF16)


def _in_proj_cast_kernel(x_ref, g_ref, w_ref, z_ref, wb_ref, h_ref):
    @pl.when(pl.program_id(0) == 0)
    def _():
        _normed_bf16(x_ref, g_ref, h_ref)

    wb_ref[...] = w_ref[...].astype(BF16)
    z_ref[...] = jnp.dot(h_ref[...], wb_ref[...], preferred_element_type=F32)


def _in_proj_window_kernel(x_ref, g_ref, w_ref, *rest, tiles_per_seq, first_win_tile, k_tile, v_tile,
                           zero_fill):
    z_ref, kt_ref, vt_ref, h_ref = rest[-4:]
    j = pl.program_id(1)
    tm = x_ref.shape[0]

    @pl.when(j == 0)
    def _():
        for row in range(0, tm, NORM_ROWS):
            rows = pl.ds(row, NORM_ROWS)
            x = x_ref[rows, :]
            y = x * lax.rsqrt(jnp.mean(x * x, axis=-1, keepdims=True) + RMS_EPS)
            h_ref[rows, :] = (y * g_ref[...]).astype(BF16)
            z_ref[rows, :] = jnp.dot(h_ref[rows, :], w_ref[...], preferred_element_type=F32)

    @pl.when(j > 0)
    def _():
        z_ref[...] = jnp.dot(h_ref[...], w_ref[...], preferred_element_type=F32)

    in_window = pl.program_id(0) % tiles_per_seq >= first_win_tile

    @pl.when(jnp.logical_and(in_window, j == k_tile))
    def _():
        kt_ref[...] = z_ref[...].T

    @pl.when(jnp.logical_and(in_window, j == v_tile))
    def _():
        vt_ref[...] = z_ref[...].T

    if zero_fill:
        @pl.when(jnp.logical_not(in_window))
        def _():
            kt_ref[...] = jnp.zeros_like(kt_ref)
            vt_ref[...] = jnp.zeros_like(vt_ref)


def _in_proj_cast(x, g, w_f32, layer, *, tn):
    m, d = x.shape
    n = w_f32.shape[2]
    return pl.pallas_call(
        _in_proj_cast_kernel,
        out_shape=(jax.ShapeDtypeStruct((m, n), F32), jax.ShapeDtypeStruct((d, n), BF16)),
        grid=(n // tn,),
        in_specs=[
            pl.BlockSpec((m, d), lambda j: (0, 0)),
            pl.BlockSpec((None, 1, d), lambda j: (layer, 0, 0)),
            pl.BlockSpec((None, d, tn), lambda j: (layer, 0, j)),
        ],
        out_specs=(pl.BlockSpec((m, tn), lambda j: (0, j)),
                   pl.BlockSpec((d, tn), lambda j: (0, j))),
        scratch_shapes=[pltpu.VMEM((m, d), BF16)],
        compiler_params=_params("arbitrary"),
        name="in_proj_sample",
    )(x, g, w_f32)


def _in_proj_window(x, g, w_bf16, layer, depth, windows, *, seq, keep, k_off, v_off, d_attn, tm):
    m, d = x.shape
    n = w_bf16.shape[1]
    tn = d_attn
    assert k_off % tn == 0 and v_off % tn == 0 and (seq - keep) % tm == 0 and seq % tm == 0
    batch = m // seq
    tiles_per_seq = seq // tm
    first_win_tile = (seq - keep) // tm
    win_tiles = tiles_per_seq - first_win_tile
    n_col = n // tn
    create = windows is None
    zero_fill = create and depth > 1
    n_slots = batch * first_win_tile * n_col
    n_zero = (depth - 1) * batch * win_tiles
    assert (layer == 0) == create and (not zero_fill or n_slots >= n_zero)

    def win_map(i, j):
        b, t = i // tiles_per_seq, i % tiles_per_seq
        own = (layer, b, 0, jnp.maximum(t - first_win_tile, 0))
        if not zero_fill:
            return own
        zero_block = ((b * first_win_tile + t) * n_col + j) * n_zero // n_slots
        other = (1 + zero_block // (batch * win_tiles), (zero_block // win_tiles) % batch, 0,
                 zero_block % win_tiles)
        return tuple(jnp.where(t >= first_win_tile, o, z) for o, z in zip(own, other))

    win = jax.ShapeDtypeStruct((depth, batch, d_attn, keep), F32)
    win_spec = pl.BlockSpec((None, None, d_attn, tm), win_map)
    extra_specs = [] if create else [pl.BlockSpec(memory_space=pl.ANY)] * 2
    return pl.pallas_call(
        functools.partial(_in_proj_window_kernel, tiles_per_seq=tiles_per_seq, first_win_tile=first_win_tile,
                          k_tile=k_off // tn, v_tile=v_off // tn, zero_fill=zero_fill),
        out_shape=(jax.ShapeDtypeStruct((m, n), F32), win, win),
        grid=(m // tm, n_col),
        in_specs=[
            pl.BlockSpec((tm, d), lambda i, j: (i, 0)),
            pl.BlockSpec((None, 1, d), lambda i, j: (layer, 0, 0)),
            pl.BlockSpec((d, tn), lambda i, j: (0, j)),
            *extra_specs,
        ],
        out_specs=(pl.BlockSpec((tm, tn), lambda i, j: (i, j)), win_spec, win_spec),
        scratch_shapes=[pltpu.VMEM((tm, d), BF16)],
        input_output_aliases={} if create else {3: 1, 4: 2},
        compiler_params=_params("arbitrary", "arbitrary"),
        name="in_proj_prompt",
    )(x, g, w_bf16, *(() if create else windows))


def _rms_norm_kernel(x_ref, g_ref, o_ref):
    x = x_ref[...]
    y = x * lax.rsqrt(jnp.mean(x * x, axis=-1, keepdims=True) + RMS_EPS)
    o_ref[...] = y * g_ref[...]


def _rms_norm(x, g, *, tm):
    m, d = x.shape
    return pl.pallas_call(
        _rms_norm_kernel,
        out_shape=jax.ShapeDtypeStruct((m, d), F32),
        grid=(m // tm,),
        in_specs=[pl.BlockSpec((tm, d), lambda i: (i, 0)),
                  pl.BlockSpec((1, d), lambda i: (0, 0))],
        out_specs=pl.BlockSpec((tm, d), lambda i: (i, 0)),
        compiler_params=_params("parallel"),
        name="final_norm",
    )(x, g.reshape(1, d))


def _prompt_bias_table(n_heads):
    slopes = _alibi_slopes(n_heads)
    qi = np.arange(BAND)[:, None] + BAND
    kj = np.arange(2 * BAND)[None, :]
    dist = qi - kj
    valid = (dist >= 0) & (dist <= BAND)
    valid_first = valid & (kj >= BAND)
    table = np.empty((n_heads, N_PATTERNS, 2, BAND, 2 * BAND), np.float32)
    for h in range(n_heads):
        for p, (_, d) in enumerate(DILATED_PATTERNS):
            bias = -slopes[h] * (dist * d).astype(np.float32)
            table[h, p, 0] = np.where(valid, bias, -np.inf)
            table[h, p, 1] = np.where(valid_first, bias, -np.inf)
    table = table.reshape(n_heads // HEADS_PER_TILE, HEADS_PER_TILE, N_PATTERNS, 2, BAND, 2 * BAND)
    table = table.transpose(0, 2, 3, 1, 4, 5)
    return jnp.asarray(table.reshape(n_heads // HEADS_PER_TILE, N_PATTERNS, 2, HEADS_PER_TILE * BAND, 2 * BAND))


ATTN_GROUP = 4
SOFTMAX_ROWS = 32


PREP_ROWS = 256


def _attn_prompt_kernel(q_ref, k_ref, v_ref, bias_ref, a_ref,
                        o_scr, m_scr, d_scr, qd_scrs, kd_scrs, vd_scrs,
                        group_bufs, *, seq):
    lane = lax.broadcasted_iota(jnp.int32, (BAND, LANES), 1)
    head0 = lane < HEAD_DIM
    stacked = HEADS_PER_TILE * BAND
    scale = HEAD_DIM ** -0.5
    srcs = (q_ref, k_ref, v_ref)
    assert DILATED_PATTERNS[1][1] ** 2 == DILATED_PATTERNS[2][1] and DILATED_PATTERNS[0][1] == 1
    d_mid, d_big = DILATED_PATTERNS[1][1], DILATED_PATTERNS[2][1]

    def put(ti, pi, sub, row, val):
        n_sub = seq // DILATED_PATTERNS[pi][1]
        if ti == 0:
            qd_scrs[pi][pl.ds(sub * n_sub + row, PREP_ROWS), :] = (val * scale).astype(BF16)
        else:
            dst = (kd_scrs, vd_scrs)[ti - 1][pi]
            dst[pl.ds(sub * (BAND + n_sub) + BAND + row, PREP_ROWS), :] = val.astype(BF16)

    for pi, (_, d) in enumerate(DILATED_PATTERNS):
        n_sub = seq // d
        for dst in (kd_scrs[pi], vd_scrs[pi]):
            for r in range(d):
                dst[pl.ds(r * (BAND + n_sub), BAND), :] = jnp.zeros((BAND, LANES), BF16)

    def prep_dense(c, carry):
        row = pl.multiple_of(c * PREP_ROWS, PREP_ROWS)
        for ti in range(3):
            put(ti, 0, 0, row, srcs[ti][pl.ds(row, PREP_ROWS), :])
        return carry

    lax.fori_loop(0, seq // PREP_ROWS, prep_dense, 0)

    n_mid = seq // d_mid

    def prep_mid(c, carry):
        row = pl.multiple_of(c * PREP_ROWS, PREP_ROWS)
        for ti in range(3):
            for r in range(d_mid):
                val = srcs[ti][pl.ds(r + d_mid * row, PREP_ROWS, stride=d_mid), :]
                o_scr[ti, pl.ds(r * n_mid + row, PREP_ROWS), :] = val
                put(ti, 1, r, row, val)
        return carry

    lax.fori_loop(0, n_mid // PREP_ROWS, prep_mid, 0)

    n_big = seq // d_big
    for c in range(n_big // PREP_ROWS):
        for ti in range(3):
            for r in range(d_mid):
                for a in range(d_mid):
                    val = o_scr[ti, pl.ds(r * n_mid + a + d_mid * c * PREP_ROWS, PREP_ROWS, stride=d_mid), :]
                    put(ti, 2, r + d_mid * a, c * PREP_ROWS, val)

    def aligned(row):
        return row if isinstance(row, int) else pl.multiple_of(row, BAND)

    stages = []
    for pi, (_, d) in enumerate(DILATED_PATTERNS):
        nblk = seq // d // BAND
        n_sub = seq // d
        qd, kd, vd = qd_scrs[pi], kd_scrs[pi], vd_scrs[pi]

        def units_of(grp, d=d, nblk=nblk, n_sub=n_sub):
            units = []
            for g in range(ATTN_GROUP):
                idx = grp * ATTN_GROUP + g
                r = idx // nblk
                blk = idx % nblk
                q_row = aligned(r * n_sub + blk * BAND)
                k_row = aligned(r * (BAND + n_sub) + blk * BAND)
                if d == d_big:
                    out_rows = pl.ds((r % d_mid) * n_mid + r // d_mid + d_mid * BAND * blk, BAND, stride=d_mid)
                else:
                    out_rows = pl.ds(q_row, BAND)
                first = int(blk == 0) if isinstance(blk, int) else jnp.asarray(blk == 0, jnp.int32)
                units.append((q_row, k_row, out_rows, first))
            return units

        def scores(units, bufs, pi=pi, qd=qd, kd=kd):
            s_buf, _, mg_buf, _ = bufs
            for g, (q_row, k_row, _, first) in enumerate(units):
                q = qd[pl.ds(q_row, BAND), :]
                zero = jnp.zeros_like(q)
                q2 = jnp.concatenate([jnp.where(head0, q, zero), jnp.where(head0, zero, q)], axis=0)
                s = lax.dot_general(q2, kd[pl.ds(k_row, 2 * BAND), :], NT_DIMS,
                                    preferred_element_type=F32) + bias_ref[pi, first]
                s_buf[g] = s
                mg_buf[g] = jnp.broadcast_to(jnp.max(s, axis=-1, keepdims=True), (stacked, LANES))

        def softmax_values(units, bufs, pi=pi, vd=vd):
            s_buf, p_buf, mg_buf, dg_buf = bufs
            for g in range(ATTN_GROUP):
                for c in range(stacked // SOFTMAX_ROWS):
                    sl = pl.ds(c * SOFTMAX_ROWS, SOFTMAX_ROWS)
                    m = mg_buf[g, sl, :]
                    p = jnp.exp(s_buf[g, sl, :] - jnp.concatenate([m, m], axis=-1))
                    p_buf[g, sl, :] = p.astype(BF16)
                    dg_buf[g, sl, :] = jnp.broadcast_to(jnp.sum(p, axis=-1, keepdims=True), (SOFTMAX_ROWS, LANES))
            for g, (_, _, out_rows, _) in enumerate(units):
                m_scr[pi, out_rows, :] = jnp.where(head0, mg_buf[g, pl.ds(0, BAND), :], mg_buf[g, pl.ds(BAND, BAND), :])
                d_scr[pi, out_rows, :] = jnp.where(head0, dg_buf[g, pl.ds(0, BAND), :], dg_buf[g, pl.ds(BAND, BAND), :])
            for g, (_, k_row, out_rows, _) in enumerate(units):
                o2 = jnp.dot(p_buf[g], vd[pl.ds(k_row, 2 * BAND), :], preferred_element_type=F32)
                o_scr[pi, out_rows, :] = jnp.where(head0, o2[:BAND], o2[BAND:])

        stages.append((units_of, scores, softmax_values, d * nblk // ATTN_GROUP))

    def pair(j, units_of, scores, softmax_values, score_ahead):
        even, odd = units_of(2 * j), units_of(2 * j + 1)
        scores(odd, group_bufs[1])
        softmax_values(even, group_bufs[0])
        score_ahead()
        softmax_values(odd, group_bufs[1])

    stages[0][1](stages[0][0](0), group_bufs[0])
    for pi, (units_of, scores, softmax_values, n_groups) in enumerate(stages):
        def body(j, carry, units_of=units_of, scores=scores, softmax_values=softmax_values):
            pair(j, units_of, scores, softmax_values, lambda: scores(units_of(2 * j + 2), group_bufs[0]))
            return carry

        lax.fori_loop(0, n_groups // 2 - 1, body, 0)
        if pi + 1 < len(stages):
            next_units_of, next_scores = stages[pi + 1][:2]
            score_ahead = functools.partial(next_scores, next_units_of(0), group_bufs[0])
        else:
            score_ahead = lambda: None
        pair(n_groups // 2 - 1, units_of, scores, softmax_values, score_ahead)

    chunk = 2 * BAND

    def combine(c, carry):
        start = pl.multiple_of(c * chunk, chunk)
        sl = pl.ds(start, chunk)
        natural = pl.ds(start // n_mid + d_mid * (start % n_mid), chunk, stride=d_mid)
        ms = [m_scr[0, natural, :], m_scr[1, sl, :], m_scr[2, sl, :]]
        m = jnp.maximum(jnp.maximum(ms[0], ms[1]), ms[2])
        es = [jnp.exp(mi - m) for mi in ms]
        num = es[0] * o_scr[0, natural, :] + es[1] * o_scr[1, sl, :] + es[2] * o_scr[2, sl, :]
        den = es[0] * d_scr[0, natural, :] + es[1] * d_scr[1, sl, :] + es[2] * d_scr[2, sl, :]
        a_ref[natural, :] = num / den
        return carry

    lax.fori_loop(0, seq // chunk, combine, 0)


def _attn_prompt(z, bias, *, batch, seq, q_off, k_off, v_off, d_attn):
    n_pairs = d_attn // LANES
    stacked = HEADS_PER_TILE * BAND

    def col(off):
        return pl.BlockSpec((seq, LANES), lambda b, hp: (b, off // LANES + hp))

    return pl.pallas_call(
        functools.partial(_attn_prompt_kernel, seq=seq),
        out_shape=jax.ShapeDtypeStruct((batch * seq, d_attn), F32),
        grid=(batch, n_pairs),
        in_specs=[
            col(q_off), col(k_off), col(v_off),
            pl.BlockSpec((None, N_PATTERNS, 2, stacked, 2 * BAND), lambda b, hp: (hp, 0, 0, 0, 0)),
        ],
        out_specs=pl.BlockSpec((seq, LANES), lambda b, hp: (b, hp)),
        scratch_shapes=[pltpu.VMEM((N_PATTERNS, seq, LANES), F32),
                        pltpu.VMEM((N_PATTERNS, seq, LANES), F32),
                        pltpu.VMEM((N_PATTERNS, seq, LANES), F32),
                        [pltpu.VMEM((seq, LANES), BF16) for _ in DILATED_PATTERNS],
                        [pltpu.VMEM((seq + d * BAND, LANES), BF16) for _, d in DILATED_PATTERNS],
                        [pltpu.VMEM((seq + d * BAND, LANES), BF16) for _, d in DILATED_PATTERNS],
                        [[pltpu.VMEM((ATTN_GROUP, stacked, 2 * BAND), F32),
                          pltpu.VMEM((ATTN_GROUP, stacked, 2 * BAND), BF16),
                          pltpu.VMEM((ATTN_GROUP, stacked, LANES), F32),
                          pltpu.VMEM((ATTN_GROUP, stacked, LANES), F32)]
                         for _ in range(2)]],
        compiler_params=_params("parallel", "parallel"),
        name="attn_prompt",
    )(z, z, z, bias)


def _pooled_window(ext_ref, row0, n_rows, cols, w, pos):
    cur = ext_ref[pl.ds(row0, n_rows), cols]
    acc = cur
    for i in range(1, w):
        acc = acc + ext_ref[pl.ds(row0 - i, n_rows), cols]
    return acc / jnp.minimum(w, pos + 1).astype(F32) - cur


POOL_ROWS = 64


def _prompt_mix(u_ref, halo_ref, gp_ref, ga_ref, a_ref, pw_ref, ps_ref, ext_ref, level_refs, pooled_ref, mix_ref,
                *, tile, tm, d_pool):
    halo = halo_ref.shape[0]
    n_grp = len(POOL_WINDOWS)
    group = d_pool // n_grp
    assert all(w == 2 << g for g, w in enumerate(POOL_WINDOWS)) and halo >= (n_grp) * SUBLANES
    ext_ref[pl.ds(0, halo), :] = jnp.where(tile == 0, 0.0, halo_ref[...])
    ext_ref[pl.ds(halo, tm), :] = u_ref[...]
    below, below_cols = ext_ref, slice(0, d_pool)
    for g, w in enumerate(POOL_WINDOWS):
        shift = w // 2
        level = level_refs[g] if g + 1 < n_grp else None
        own_cols = slice(g * group, (g + 1) * group)

        def window_sum(row, n, below=below, below_cols=below_cols, shift=shift):
            return below[pl.ds(row, n), below_cols] + below[pl.ds(row - shift, n), below_cols]

        first = (g + 1) * SUBLANES
        if level is not None and first < halo:
            level[pl.ds(first, halo - first), :] = window_sum(first, halo - first)
        for row in range(0, tm, POOL_ROWS):
            s = window_sum(halo + row, POOL_ROWS)
            if level is not None:
                level[pl.ds(halo + row, POOL_ROWS), :] = s
            pos = tile * tm + row + lax.broadcasted_iota(jnp.int32, (POOL_ROWS, 1), 0)
            cnt = jnp.minimum(w, pos + 1).astype(F32)
            pooled = s[:, :group] / cnt - ext_ref[pl.ds(halo + row, POOL_ROWS), own_cols]
            pooled_ref[pl.ds(row, POOL_ROWS), own_cols] = pooled.astype(BF16)
        mixed = jnp.dot(pooled_ref[:, own_cols], pw_ref[g], preferred_element_type=F32) * ps_ref[:, own_cols]
        mix_ref[:, own_cols] = (mixed * _silu(gp_ref[:, own_cols])).astype(BF16)
        if level is not None:
            below, below_cols = level, slice(group, d_pool - g * group)
    mix_ref[:, d_pool:] = (a_ref[...] * _silu(ga_ref[...])).astype(BF16)


def _out_proj_kernel(x_ref, u_ref, halo_ref, gp_ref, ga_ref, a_ref, pw_ref, ps_ref, wo_ref, gf_ref,
                     o_ref, ext_ref, level_refs, pooled_ref, mix_ref, *, tm, tiles_per_seq, d_pool, final_norm):
    tile = pl.program_id(0) % tiles_per_seq
    _prompt_mix(u_ref, halo_ref, gp_ref, ga_ref, a_ref, pw_ref, ps_ref, ext_ref, level_refs, pooled_ref, mix_ref,
                tile=tile, tm=tm, d_pool=d_pool)
    acc = x_ref[...] + jnp.dot(mix_ref[...], wo_ref[...], preferred_element_type=F32)
    if final_norm:
        acc = acc * lax.rsqrt(jnp.mean(acc * acc, axis=-1, keepdims=True) + RMS_EPS) * gf_ref[...]
    o_ref[...] = acc


def _out_proj(x, z, a, pool_w_bf16, pool_scale, w_out_bf16, layer, final_g, *, seq, tm, d_pool, ga_off,
              final_norm):
    m, d = x.shape
    n_grp = len(POOL_WINDOWS)
    halo = n_grp * SUBLANES
    grp = d_pool // n_grp
    d_attn = a.shape[1]
    assert ga_off % d_attn == 0 and d_pool == d_attn and halo >= POOL_MAX - 1 and tm % POOL_ROWS == 0
    return pl.pallas_call(
        functools.partial(_out_proj_kernel, tm=tm, tiles_per_seq=seq // tm, d_pool=d_pool,
                          final_norm=final_norm),
        out_shape=jax.ShapeDtypeStruct((m, d), F32),
        grid=(m // tm,),
        in_specs=[
            pl.BlockSpec((tm, d), lambda i: (i, 0)),
            pl.BlockSpec((tm, d_pool), lambda i: (i, 0)),
            pl.BlockSpec((halo, d_pool), lambda i: (jnp.maximum(i * (tm // halo) - 1, 0), 0)),
            pl.BlockSpec((tm, d_pool), lambda i: (i, 1)),
            pl.BlockSpec((tm, d_attn), lambda i: (i, ga_off // d_attn)),
            pl.BlockSpec((tm, d_attn), lambda i: (i, 0)),
            pl.BlockSpec((None, n_grp, grp, grp), lambda i: (layer, 0, 0, 0)),
            pl.BlockSpec((None, 1, d_pool), lambda i: (layer, 0, 0)),
            pl.BlockSpec((d_pool + d_attn, d), lambda i: (0, 0), pipeline_mode=pl.Buffered(1)),
            pl.BlockSpec((1, d), lambda i: (0, 0)),
        ],
        out_specs=pl.BlockSpec((tm, d), lambda i: (i, 0)),
        scratch_shapes=[pltpu.VMEM((halo + tm, d_pool), F32),
                        [pltpu.VMEM((halo + tm, d_pool - g * grp), F32) for g in range(n_grp - 1)],
                        pltpu.VMEM((tm, d_pool), BF16),
                        pltpu.VMEM((tm, d_pool + d_attn), BF16)],
        compiler_params=_params("parallel"),
        name="out_proj",
    )(x, z, z, z, z, a, pool_w_bf16, pool_scale, w_out_bf16, final_g.reshape(1, d))


def _sample_bias_table(n_heads, l_win):
    slopes = _alibi_slopes(n_heads)
    dist = l_win - np.arange(l_win)
    table = np.empty((n_heads, SUBLANES, l_win), np.float32)
    for p, (window, d) in enumerate(DILATED_PATTERNS):
        valid = (dist % d == 0) & (dist <= window)
        table[:, p, :] = np.where(valid[None, :], -slopes[:, None] * dist[None, :].astype(np.float32), -np.inf)
    table[:, N_PATTERNS:, :] = table[:, :1, :]
    return jnp.asarray(table)


def _sample_mix_kernel(z_ref, st_ref, kt_ref, vt_ref, bias_ref, pw_ref, ps_ref,
                       mix_ref, st_out_ref, ext_ref, a_scr, *, d_pool, d_attn, pos0):
    n_heads = d_attn // HEAD_DIM
    u = z_ref[:, 0:d_pool]
    gp = z_ref[:, d_pool:2 * d_pool]
    ga = z_ref[:, 2 * d_pool + 3 * d_attn:2 * d_pool + 4 * d_attn]
    q_off, k_off, v_off = 2 * d_pool, 2 * d_pool + d_attn, 2 * d_pool + 2 * d_attn

    n_state = POOL_MAX - 1
    ext_ref[pl.ds(0, 1), :] = jnp.zeros((1, d_pool), F32)
    ext_ref[pl.ds(1, n_state), :] = st_ref[...]
    ext_ref[pl.ds(POOL_MAX, 1), :] = u
    pos = jnp.full((1, 1), pos0, jnp.int32)
    group = d_pool // len(POOL_WINDOWS)
    parts = []
    for g, w in enumerate(POOL_WINDOWS):
        cols = slice(g * group, (g + 1) * group)
        pooled = _pooled_window(ext_ref, POOL_MAX, 1, cols, w, pos)
        parts.append(jnp.dot(pooled.astype(BF16), pw_ref[g], preferred_element_type=F32))
    pm = jnp.concatenate(parts, axis=-1) * ps_ref[...]
    st_out_ref[pl.ds(0, n_state - 1), :] = st_ref[pl.ds(1, n_state - 1), :]
    st_out_ref[pl.ds(n_state - 1, 1), :] = u

    is_pattern = lax.broadcasted_iota(jnp.int32, (SUBLANES, 1), 0) < N_PATTERNS
    for h in range(n_heads):
        cols = slice(h * HEAD_DIM, (h + 1) * HEAD_DIM)

        def head_row(off, cols=cols):
            return z_ref[:, off:off + d_attn][:, cols]

        q8 = jnp.broadcast_to(head_row(q_off) * (HEAD_DIM ** -0.5), (SUBLANES, HEAD_DIM)).astype(BF16)
        k_new = head_row(k_off).astype(BF16).astype(F32)
        v_new = head_row(v_off).astype(BF16).astype(F32)
        s = jnp.dot(q8, kt_ref[h].astype(BF16), preferred_element_type=F32) + bias_ref[h]
        s_new = jnp.sum(q8.astype(F32) * k_new, axis=-1, keepdims=True)
        m = jnp.maximum(jnp.max(s, axis=-1, keepdims=True), s_new)
        p = jnp.exp(s - m)
        p_new = jnp.exp(s_new - m)
        den = jnp.sum(p, axis=-1, keepdims=True) + p_new
        o = lax.dot_general(p.astype(BF16), vt_ref[h].astype(BF16), NT_DIMS, preferred_element_type=F32)
        o = (o + p_new.astype(BF16).astype(F32) * v_new) / den
        lse = jnp.where(is_pattern, m + jnp.log(den), -jnp.inf)
        e = jnp.exp(lse - jnp.max(lse, axis=0, keepdims=True))
        a_scr[:, cols] = jnp.sum(e * o, axis=0, keepdims=True) / jnp.sum(e, axis=0, keepdims=True)

    mix_ref[:, :d_pool] = pm * _silu(gp)
    mix_ref[:, d_pool:] = a_scr[...] * _silu(ga)


def _sample_mix(z, state, kt_cache, vt_cache, layer, bias, pool_w_bf16, pool_scale, *, d_pool, d_attn, pos0):
    bd, n_state, _ = state.shape
    n_heads, _, l_win = kt_cache.shape[2:]
    n_in = z.shape[1]
    n_grp = len(POOL_WINDOWS)
    grp = d_pool // n_grp
    cache_spec = pl.BlockSpec((None, None, n_heads, HEAD_DIM, l_win), lambda b: (layer, b, 0, 0, 0))
    return pl.pallas_call(
        functools.partial(_sample_mix_kernel, d_pool=d_pool, d_attn=d_attn, pos0=pos0),
        out_shape=(jax.ShapeDtypeStruct((bd, 1, d_pool + d_attn), F32),
                   jax.ShapeDtypeStruct((bd, n_state, d_pool), F32)),
        grid=(bd,),
        in_specs=[
            pl.BlockSpec((None, 1, n_in), lambda b: (b, 0, 0)),
            pl.BlockSpec((None, n_state, d_pool), lambda b: (b, 0, 0)),
            cache_spec, cache_spec,
            pl.BlockSpec((n_heads, SUBLANES, l_win), lambda b: (0, 0, 0)),
            pl.BlockSpec((None, n_grp, grp, grp), lambda b: (layer, 0, 0, 0)),
            pl.BlockSpec((None, 1, d_pool), lambda b: (layer, 0, 0)),
        ],
        out_specs=(pl.BlockSpec((None, 1, d_pool + d_attn), lambda b: (b, 0, 0)),
                   pl.BlockSpec((None, n_state, d_pool), lambda b: (b, 0, 0))),
        scratch_shapes=[pltpu.VMEM((POOL_MAX + SUBLANES, d_pool), F32),
                        pltpu.VMEM((1, d_attn), F32)],
        compiler_params=_params("parallel"),
        name="sample_mix",
    )(z.reshape(bd, 1, n_in), state, kt_cache, vt_cache, bias, pool_w_bf16, pool_scale)


def _resid_matmul_cast_kernel(x_ref, mix_ref, w_ref, o_ref, wb_ref):
    wb_ref[...] = w_ref[...].astype(BF16)
    o_ref[...] = x_ref[...] + jnp.dot(mix_ref[...].astype(BF16), wb_ref[...], preferred_element_type=F32)


def _resid_matmul_cast(x, mix, w_f32, layer, *, tn):
    m, d = x.shape
    kdim = mix.shape[1]
    return pl.pallas_call(
        _resid_matmul_cast_kernel,
        out_shape=(jax.ShapeDtypeStruct((m, d), F32), jax.ShapeDtypeStruct((kdim, d), BF16)),
        grid=(d // tn,),
        in_specs=[pl.BlockSpec((m, tn), lambda j: (0, j)),
                  pl.BlockSpec((m, kdim), lambda j: (0, 0)),
                  pl.BlockSpec((None, kdim, tn), lambda j: (layer, 0, j))],
        out_specs=(pl.BlockSpec((m, tn), lambda j: (0, j)),
                   pl.BlockSpec((kdim, tn), lambda j: (0, j))),
        compiler_params=_params("parallel"),
        name="sample_out_proj",
    )(x, mix, w_f32)


def kernel(x_prompt, x_sample, state_pool, cache_k_win, cache_v_win, norm_g, w_in, pool_w, pool_scale,
           w_out, final_norm_g):
    batch, seq, d_model = x_prompt.shape
    bd, dec_seq, _ = x_sample.shape
    depth = norm_g.shape[0]
    d_pool = pool_scale.shape[1]
    d_attn = w_out.shape[1] - d_pool
    n_heads = d_attn // HEAD_DIM
    l_win = cache_k_win.shape[2]
    assert dec_seq == 1 and l_win == WIN_MAX and PAST_LEN + 1 >= POOL_MAX
    assert seq % (BAND * max(d for _, d in DILATED_PATTERNS)) == 0 and seq >= WIN_MAX
    q_off, k_off, v_off, ga_off = (2 * d_pool, 2 * d_pool + d_attn, 2 * d_pool + 2 * d_attn,
                                   2 * d_pool + 3 * d_attn)

    tiles = _tile_plan(seq, d_model, d_attn)
    pool_w_b = pool_w.astype(BF16)
    gains = norm_g.reshape(depth, 1, d_model)
    scales = pool_scale.reshape(depth, 1, d_pool)
    bias_p = _prompt_bias_table(n_heads)
    bias_s = _sample_bias_table(n_heads, l_win)

    kt_cache = cache_k_win.transpose(0, 1, 3, 4, 2)
    vt_cache = cache_v_win.transpose(0, 1, 3, 4, 2)
    xs = x_sample.reshape(bd, d_model)
    pool_s, k_s, v_s, w_in_b, w_out_b = [], [], [], [], []
    for l in range(depth):
        z, wb = _in_proj_cast(xs, gains, w_in, l, tn=tiles["sample_cols"])
        w_in_b.append(wb)
        mix, st = _sample_mix(z, state_pool[l], kt_cache, vt_cache, l, bias_s, pool_w_b, scales,
                              d_pool=d_pool, d_attn=d_attn, pos0=PAST_LEN)
        xs, wb = _resid_matmul_cast(xs, mix.reshape(bd, -1), w_out, l, tn=tiles["sample_cols"])
        w_out_b.append(wb)
        pool_s.append(st)
        k_s.append(z[:, k_off:k_off + d_attn].reshape(bd, 1, n_heads, HEAD_DIM))
        v_s.append(z[:, v_off:v_off + d_attn].reshape(bd, 1, n_heads, HEAD_DIM))
    y_sample = _rms_norm(xs, final_norm_g, tm=bd).reshape(bd, 1, d_model)

    keep = min(WIN_MAX, seq)
    xp = x_prompt.reshape(batch * seq, d_model)
    pool_p, windows = [], None
    for l in range(depth):
        z, *windows = _in_proj_window(xp, gains, w_in_b[l], l, depth, windows, seq=seq, keep=keep,
                                      k_off=k_off, v_off=v_off, d_attn=d_attn, tm=tiles["in_proj_rows"])
        a = _attn_prompt(z, bias_p, batch=batch, seq=seq, q_off=q_off, k_off=k_off, v_off=v_off, d_attn=d_attn)
        xp = _out_proj(xp, z, a, pool_w_b, scales, w_out_b[l], l, final_norm_g, seq=seq, tm=tiles["out_proj_rows"],
                       d_pool=d_pool, ga_off=ga_off, final_norm=(l == depth - 1))
        pool_p.append(z.reshape(batch, seq, -1)[:, seq - (POOL_MAX - 1):, :d_pool])
    y_prompt = xp.reshape(batch, seq, d_model)

    def from_slabs(slabs):
        return slabs.reshape(depth, batch, n_heads, HEAD_DIM, keep).transpose(0, 1, 4, 2, 3)

    return (y_prompt, y_sample,
            jnp.stack(pool_p), from_slabs(windows[0]), from_slabs(windows[1]),
            jnp.stack(pool_s), jnp.stack(k_s), jnp.stack(v_s))
```

```python
import functools

import numpy as np
import jax
import jax.numpy as jnp
from jax import lax
from jax.experimental import pallas as pl
from jax.experimental.pallas import tpu as pltpu

HEAD_DIM = 64
LANES = 128
SUBLANES = 8
HEADS_PER_TILE = LANES // HEAD_DIM
POOL_WINDOWS = (2, 4, 8, 16)
POOL_MAX = max(POOL_WINDOWS)
DILATED_PATTERNS = ((128, 1), (512, 4), (2048, 16))
N_PATTERNS = len(DILATED_PATTERNS)
BAND = 128
WIN_MAX = max(w for w, _ in DILATED_PATTERNS)
PAST_LEN = 16384
RMS_EPS = 1e-6
VMEM_LIMIT_BYTES = 56 * 2**20

F32 = jnp.float32
BF16 = jnp.bfloat16
NT_DIMS = (((1,), (1,)), ((), ()))


def _alibi_slopes(n_heads):
    return (2.0 ** (-8.0 * np.arange(1, n_heads + 1) / n_heads)).astype(np.float32)


def _silu(x):
    return x * (1.0 / (1.0 + jnp.exp(-x)))


def _params(*sem):
    return pltpu.CompilerParams(dimension_semantics=sem, vmem_limit_bytes=VMEM_LIMIT_BYTES)


def _tile_plan(seq, d_model, d_attn):
    in_proj_rows = min(1024, seq)
    out_proj_rows = min(512, seq)
    in_proj_bytes = (2 * 4 + 2) * in_proj_rows * d_model + 2 * 2 * d_model * d_attn + 3 * 2 * 4 * in_proj_rows * d_attn
    out_proj_bytes = 2 * d_model * d_model + (2 * 2 + 4 * 2 // 2) * 4 * out_proj_rows * d_model
    assert max(in_proj_bytes, out_proj_bytes) <= VMEM_LIMIT_BYTES
    return dict(in_proj_rows=in_proj_rows, out_proj_rows=out_proj_rows, sample_cols=d_attn)


NORM_ROWS = 256


def _normed_bf16(x_ref, g_ref, h_ref):
    x = x_ref[...]
    y = x * lax.rsqrt(jnp.mean(x * x, axis=-1, keepdims=True) + RMS_EPS)
    h_ref[...] = (y * g_ref[...]).astype(BF16)


def _in_proj_cast_kernel(x_ref, g_ref, w_ref, z_ref, wb_ref, h_ref):
    @pl.when(pl.program_id(0) == 0)
    def _():
        _normed_bf16(x_ref, g_ref, h_ref)

    wb_ref[...] = w_ref[...].astype(BF16)
    z_ref[...] = jnp.dot(h_ref[...], wb_ref[...], preferred_element_type=F32)


def _in_proj_window_kernel(x_ref, g_ref, w_ref, *rest, tiles_per_seq, first_win_tile, k_tile, v_tile,
                           zero_fill):
    z_ref, kt_ref, vt_ref, h_ref = rest[-4:]
    j = pl.program_id(1)
    tm = x_ref.shape[0]

    @pl.when(j == 0)
    def _():
        for row in range(0, tm, NORM_ROWS):
            rows = pl.ds(row, NORM_ROWS)
            x = x_ref[rows, :]
            y = x * lax.rsqrt(jnp.mean(x * x, axis=-1, keepdims=True) + RMS_EPS)
            h_ref[rows, :] = (y * g_ref[...]).astype(BF16)
            z_ref[rows, :] = jnp.dot(h_ref[rows, :], w_ref[...], preferred_element_type=F32)

    in_window = pl.program_id(0) % tiles_per_seq >= first_win_tile
    assert k_tile > 0 and v_tile > 0
    to_window = jnp.logical_and(in_window, jnp.logical_or(j == k_tile, j == v_tile))

    @pl.when(jnp.logical_and(j > 0, jnp.logical_not(to_window)))
    def _():
        z_ref[...] = jnp.dot(h_ref[...], w_ref[...], preferred_element_type=F32)

    for tile_j, win_ref in ((k_tile, kt_ref), (v_tile, vt_ref)):
        @pl.when(jnp.logical_and(in_window, j == tile_j))
        def _(win_ref=win_ref):
            for row in range(0, tm, NORM_ROWS):
                rows = pl.ds(row, NORM_ROWS)
                zc = jnp.dot(h_ref[rows, :], w_ref[...], preferred_element_type=F32)
                z_ref[rows, :] = zc
                win_ref[:, rows] = zc.T

    if zero_fill:
        @pl.when(jnp.logical_not(in_window))
        def _():
            kt_ref[...] = jnp.zeros_like(kt_ref)
            vt_ref[...] = jnp.zeros_like(vt_ref)


def _in_proj_cast(x, g, w_f32, layer, *, tn):
    m, d = x.shape
    n = w_f32.shape[2]
    return pl.pallas_call(
        _in_proj_cast_kernel,
        out_shape=(jax.ShapeDtypeStruct((m, n), F32), jax.ShapeDtypeStruct((d, n), BF16)),
        grid=(n // tn,),
        in_specs=[
            pl.BlockSpec((m, d), lambda j: (0, 0)),
            pl.BlockSpec((None, 1, d), lambda j: (layer, 0, 0)),
            pl.BlockSpec((None, d, tn), lambda j: (layer, 0, j)),
        ],
        out_specs=(pl.BlockSpec((m, tn), lambda j: (0, j)),
                   pl.BlockSpec((d, tn), lambda j: (0, j))),
        scratch_shapes=[pltpu.VMEM((m, d), BF16)],
        compiler_params=_params("arbitrary"),
        name="in_proj_sample",
    )(x, g, w_f32)


def _in_proj_window(x, g, w_bf16, layer, depth, windows, *, seq, keep, k_off, v_off, d_attn, tm):
    m, d = x.shape
    n = w_bf16.shape[1]
    tn = d_attn
    assert k_off % tn == 0 and v_off % tn == 0 and (seq - keep) % tm == 0 and seq % tm == 0
    batch = m // seq
    tiles_per_seq = seq // tm
    first_win_tile = (seq - keep) // tm
    win_tiles = tiles_per_seq - first_win_tile
    n_col = n // tn
    create = windows is None
    zero_fill = create and depth > 1
    n_slots = batch * first_win_tile * n_col
    n_zero = (depth - 1) * batch * win_tiles
    assert (layer == 0) == create and (not zero_fill or n_slots >= n_zero)

    def win_map(i, j):
        b, t = i // tiles_per_seq, i % tiles_per_seq
        own = (layer, b, 0, jnp.maximum(t - first_win_tile, 0))
        if not zero_fill:
            return own
        zero_block = ((b * first_win_tile + t) * n_col + j) * n_zero // n_slots
        other = (1 + zero_block // (batch * win_tiles), (zero_block // win_tiles) % batch, 0,
                 zero_block % win_tiles)
        return tuple(jnp.where(t >= first_win_tile, o, z) for o, z in zip(own, other))

    win = jax.ShapeDtypeStruct((depth, batch, d_attn, keep), F32)
    win_spec = pl.BlockSpec((None, None, d_attn, tm), win_map)
    extra_specs = [] if create else [pl.BlockSpec(memory_space=pl.ANY)] * 2
    return pl.pallas_call(
        functools.partial(_in_proj_window_kernel, tiles_per_seq=tiles_per_seq, first_win_tile=first_win_tile,
                          k_tile=k_off // tn, v_tile=v_off // tn, zero_fill=zero_fill),
        out_shape=(jax.ShapeDtypeStruct((m, n), F32), win, win),
        grid=(m // tm, n_col),
        in_specs=[
            pl.BlockSpec((tm, d), lambda i, j: (i, 0)),
            pl.BlockSpec((None, 1, d), lambda i, j: (layer, 0, 0)),
            pl.BlockSpec((d, tn), lambda i, j: (0, j)),
            *extra_specs,
        ],
        out_specs=(pl.BlockSpec((tm, tn), lambda i, j: (i, j)), win_spec, win_spec),
        scratch_shapes=[pltpu.VMEM((tm, d), BF16)],
        input_output_aliases={} if create else {3: 1, 4: 2},
        compiler_params=_params("arbitrary", "arbitrary"),
        name="in_proj_prompt",
    )(x, g, w_bf16, *(() if create else windows))


def _rms_norm_kernel(x_ref, g_ref, o_ref):
    x = x_ref[...]
    y = x * lax.rsqrt(jnp.mean(x * x, axis=-1, keepdims=True) + RMS_EPS)
    o_ref[...] = y * g_ref[...]


def _rms_norm(x, g, *, tm):
    m, d = x.shape
    return pl.pallas_call(
        _rms_norm_kernel,
        out_shape=jax.ShapeDtypeStruct((m, d), F32),
        grid=(m // tm,),
        in_specs=[pl.BlockSpec((tm, d), lambda i: (i, 0)),
                  pl.BlockSpec((1, d), lambda i: (0, 0))],
        out_specs=pl.BlockSpec((tm, d), lambda i: (i, 0)),
        compiler_params=_params("parallel"),
        name="final_norm",
    )(x, g.reshape(1, d))


def _prompt_bias_table(n_heads):
    slopes = _alibi_slopes(n_heads)
    qi = np.arange(BAND)[:, None] + BAND
    kj = np.arange(2 * BAND)[None, :]
    dist = qi - kj
    valid = (dist >= 0) & (dist <= BAND)
    valid_first = valid & (kj >= BAND)
    table = np.empty((n_heads, N_PATTERNS, 2, BAND, 2 * BAND), np.float32)
    for h in range(n_heads):
        for p, (_, d) in enumerate(DILATED_PATTERNS):
            bias = -slopes[h] * (dist * d).astype(np.float32)
            table[h, p, 0] = np.where(valid, bias, -np.inf)
            table[h, p, 1] = np.where(valid_first, bias, -np.inf)
    table = table.reshape(n_heads // HEADS_PER_TILE, HEADS_PER_TILE, N_PATTERNS, 2, BAND, 2 * BAND)
    table = table.transpose(0, 2, 3, 1, 4, 5)
    return jnp.asarray(table.reshape(n_heads // HEADS_PER_TILE, N_PATTERNS, 2, HEADS_PER_TILE * BAND, 2 * BAND))


ATTN_GROUP = 4
SOFTMAX_ROWS = 32


PREP_ROWS = 256


def _attn_prompt_kernel(q_ref, k_ref, v_ref, bias_ref, a_ref,
                        o_scr, m_scr, d_scr, qd_scrs, kd_scrs, vd_scrs,
                        group_bufs, *, seq):
    lane = lax.broadcasted_iota(jnp.int32, (BAND, LANES), 1)
    head0 = lane < HEAD_DIM
    stacked = HEADS_PER_TILE * BAND
    scale = HEAD_DIM ** -0.5
    srcs = (q_ref, k_ref, v_ref)
    assert DILATED_PATTERNS[1][1] ** 2 == DILATED_PATTERNS[2][1] and DILATED_PATTERNS[0][1] == 1
    d_mid, d_big = DILATED_PATTERNS[1][1], DILATED_PATTERNS[2][1]

    def put(ti, pi, sub, row, val):
        n_sub = seq // DILATED_PATTERNS[pi][1]
        if ti == 0:
            qd_scrs[pi][pl.ds(sub * n_sub + row, PREP_ROWS), :] = (val * scale).astype(BF16)
        else:
            dst = (kd_scrs, vd_scrs)[ti - 1][pi]
            dst[pl.ds(sub * (BAND + n_sub) + BAND + row, PREP_ROWS), :] = val.astype(BF16)

    for pi, (_, d) in enumerate(DILATED_PATTERNS):
        n_sub = seq // d
        for dst in (kd_scrs[pi], vd_scrs[pi]):
            for r in range(d):
                dst[pl.ds(r * (BAND + n_sub), BAND), :] = jnp.zeros((BAND, LANES), BF16)

    def prep_dense(c, carry):
        row = pl.multiple_of(c * PREP_ROWS, PREP_ROWS)
        for ti in range(3):
            put(ti, 0, 0, row, srcs[ti][pl.ds(row, PREP_ROWS), :])
        return carry

    lax.fori_loop(0, seq // PREP_ROWS, prep_dense, 0)

    n_mid = seq // d_mid

    def prep_mid(c, carry):
        row = pl.multiple_of(c * PREP_ROWS, PREP_ROWS)
        for ti in range(3):
            for r in range(d_mid):
                val = srcs[ti][pl.ds(r + d_mid * row, PREP_ROWS, stride=d_mid), :]
                o_scr[ti, pl.ds(r * n_mid + row, PREP_ROWS), :] = val
                put(ti, 1, r, row, val)
        return carry

    lax.fori_loop(0, n_mid // PREP_ROWS, prep_mid, 0)

    n_big = seq // d_big
    for c in range(n_big // PREP_ROWS):
        for ti in range(3):
            for r in range(d_mid):
                for a in range(d_mid):
                    val = o_scr[ti, pl.ds(r * n_mid + a + d_mid * c * PREP_ROWS, PREP_ROWS, stride=d_mid), :]
                    put(ti, 2, r + d_mid * a, c * PREP_ROWS, val)

    def aligned(row):
        return row if isinstance(row, int) else pl.multiple_of(row, BAND)

    stages = []
    for pi, (_, d) in enumerate(DILATED_PATTERNS):
        nblk = seq // d // BAND
        n_sub = seq // d
        qd, kd, vd = qd_scrs[pi], kd_scrs[pi], vd_scrs[pi]

        def units_of(grp, d=d, nblk=nblk, n_sub=n_sub):
            units = []
            for g in range(ATTN_GROUP):
                idx = grp * ATTN_GROUP + g
                r = idx // nblk
                blk = idx % nblk
                q_row = aligned(r * n_sub + blk * BAND)
                k_row = aligned(r * (BAND + n_sub) + blk * BAND)
                if d == d_big:
                    out_rows = pl.ds((r % d_mid) * n_mid + r // d_mid + d_mid * BAND * blk, BAND, stride=d_mid)
                else:
                    out_rows = pl.ds(q_row, BAND)
                first = int(blk == 0) if isinstance(blk, int) else jnp.asarray(blk == 0, jnp.int32)
                units.append((q_row, k_row, out_rows, first))
            return units

        def scores(units, bufs, pi=pi, qd=qd, kd=kd):
            s_buf, _, mg_buf, _ = bufs
            for g, (q_row, k_row, _, first) in enumerate(units):
                q = qd[pl.ds(q_row, BAND), :]
                zero = jnp.zeros_like(q)
                q2 = jnp.concatenate([jnp.where(head0, q, zero), jnp.where(head0, zero, q)], axis=0)
                s = lax.dot_general(q2, kd[pl.ds(k_row, 2 * BAND), :], NT_DIMS,
                                    preferred_element_type=F32) + bias_ref[pi, first]
                s_buf[g] = s
                mg_buf[g] = jnp.broadcast_to(jnp.max(s, axis=-1, keepdims=True), (stacked, LANES))

        def softmax_values(units, bufs, pi=pi, vd=vd):
            s_buf, p_buf, mg_buf, dg_buf = bufs
            for g in range(ATTN_GROUP):
                for c in range(stacked // SOFTMAX_ROWS):
                    sl = pl.ds(c * SOFTMAX_ROWS, SOFTMAX_ROWS)
                    m = mg_buf[g, sl, :]
                    p = jnp.exp(s_buf[g, sl, :] - jnp.concatenate([m, m], axis=-1))
                    p_buf[g, sl, :] = p.astype(BF16)
                    dg_buf[g, sl, :] = jnp.broadcast_to(jnp.sum(p, axis=-1, keepdims=True), (SOFTMAX_ROWS, LANES))
            for g, (_, _, out_rows, _) in enumerate(units):
                m_scr[pi, out_rows, :] = jnp.where(head0, mg_buf[g, pl.ds(0, BAND), :], mg_buf[g, pl.ds(BAND, BAND), :])
                d_scr[pi, out_rows, :] = jnp.where(head0, dg_buf[g, pl.ds(0, BAND), :], dg_buf[g, pl.ds(BAND, BAND), :])
            for g, (_, k_row, out_rows, _) in enumerate(units):
                o2 = jnp.dot(p_buf[g], vd[pl.ds(k_row, 2 * BAND), :], preferred_element_type=F32)
                o_scr[pi, out_rows, :] = jnp.where(head0, o2[:BAND], o2[BAND:])

        stages.append((units_of, scores, softmax_values, d * nblk // ATTN_GROUP))

    def pair(j, units_of, scores, softmax_values, score_ahead):
        even, odd = units_of(2 * j), units_of(2 * j + 1)
        scores(odd, group_bufs[1])
        softmax_values(even, group_bufs[0])
        score_ahead()
        softmax_values(odd, group_bufs[1])

    stages[0][1](stages[0][0](0), group_bufs[0])
    for pi, (units_of, scores, softmax_values, n_groups) in enumerate(stages):
        def body(j, carry, units_of=units_of, scores=scores, softmax_values=softmax_values):
            pair(j, units_of, scores, softmax_values, lambda: scores(units_of(2 * j + 2), group_bufs[0]))
            return carry

        lax.fori_loop(0, n_groups // 2 - 1, body, 0)
        if pi + 1 < len(stages):
            next_units_of, next_scores = stages[pi + 1][:2]
            score_ahead = functools.partial(next_scores, next_units_of(0), group_bufs[0])
        else:
            score_ahead = lambda: None
        pair(n_groups // 2 - 1, units_of, scores, softmax_values, score_ahead)

    chunk = 2 * BAND

    def combine(c, carry):
        start = pl.multiple_of(c * chunk, chunk)
        sl = pl.ds(start, chunk)
        natural = pl.ds(start // n_mid + d_mid * (start % n_mid), chunk, stride=d_mid)
        ms = [m_scr[0, natural, :], m_scr[1, sl, :], m_scr[2, sl, :]]
        m = jnp.maximum(jnp.maximum(ms[0], ms[1]), ms[2])
        es = [jnp.exp(mi - m) for mi in ms]
        num = es[0] * o_scr[0, natural, :] + es[1] * o_scr[1, sl, :] + es[2] * o_scr[2, sl, :]
        den = es[0] * d_scr[0, natural, :] + es[1] * d_scr[1, sl, :] + es[2] * d_scr[2, sl, :]
        a_ref[natural, :] = num / den
        return carry

    lax.fori_loop(0, seq // chunk, combine, 0)


def _attn_prompt(z, bias, *, batch, seq, q_off, k_off, v_off, d_attn):
    n_pairs = d_attn // LANES
    stacked = HEADS_PER_TILE * BAND

    def col(off):
        return pl.BlockSpec((seq, LANES), lambda b, hp: (b, off // LANES + hp))

    return pl.pallas_call(
        functools.partial(_attn_prompt_kernel, seq=seq),
        out_shape=jax.ShapeDtypeStruct((batch * seq, d_attn), F32),
        grid=(batch, n_pairs),
        in_specs=[
            col(q_off), col(k_off), col(v_off),
            pl.BlockSpec((None, N_PATTERNS, 2, stacked, 2 * BAND), lambda b, hp: (hp, 0, 0, 0, 0)),
        ],
        out_specs=pl.BlockSpec((seq, LANES), lambda b, hp: (b, hp)),
        scratch_shapes=[pltpu.VMEM((N_PATTERNS, seq, LANES), F32),
                        pltpu.VMEM((N_PATTERNS, seq, LANES), F32),
                        pltpu.VMEM((N_PATTERNS, seq, LANES), F32),
                        [pltpu.VMEM((seq, LANES), BF16) for _ in DILATED_PATTERNS],
                        [pltpu.VMEM((seq + d * BAND, LANES), BF16) for _, d in DILATED_PATTERNS],
                        [pltpu.VMEM((seq + d * BAND, LANES), BF16) for _, d in DILATED_PATTERNS],
                        [[pltpu.VMEM((ATTN_GROUP, stacked, 2 * BAND), F32),
                          pltpu.VMEM((ATTN_GROUP, stacked, 2 * BAND), BF16),
                          pltpu.VMEM((ATTN_GROUP, stacked, LANES), F32),
                          pltpu.VMEM((ATTN_GROUP, stacked, LANES), F32)]
                         for _ in range(2)]],
        compiler_params=_params("parallel", "parallel"),
        name="attn_prompt",
    )(z, z, z, bias)


def _pooled_window(ext_ref, row0, n_rows, cols, w, pos):
    cur = ext_ref[pl.ds(row0, n_rows), cols]
    acc = cur
    for i in range(1, w):
        acc = acc + ext_ref[pl.ds(row0 - i, n_rows), cols]
    return acc / jnp.minimum(w, pos + 1).astype(F32) - cur


POOL_ROWS = 64


def _prompt_mix(u_ref, halo_ref, gp_ref, ga_ref, a_ref, pw_ref, ps_ref, ext_ref, level_refs, pooled_ref, mix_ref,
                *, tile, tm, d_pool):
    halo = halo_ref.shape[0]
    n_grp = len(POOL_WINDOWS)
    group = d_pool // n_grp
    assert all(w == 2 << g for g, w in enumerate(POOL_WINDOWS)) and halo >= (n_grp) * SUBLANES
    ext_ref[pl.ds(0, halo), :] = jnp.where(tile == 0, 0.0, halo_ref[...])
    ext_ref[pl.ds(halo, tm), :] = u_ref[...]
    below, below_cols = ext_ref, slice(0, d_pool)
    for g, w in enumerate(POOL_WINDOWS):
        shift = w // 2
        level = level_refs[g] if g + 1 < n_grp else None
        own_cols = slice(g * group, (g + 1) * group)

        def window_sum(row, n, below=below, below_cols=below_cols, shift=shift):
            return below[pl.ds(row, n), below_cols] + below[pl.ds(row - shift, n), below_cols]

        first = (g + 1) * SUBLANES
        if level is not None and first < halo:
            level[pl.ds(first, halo - first), :] = window_sum(first, halo - first)
        for row in range(0, tm, POOL_ROWS):
            s = window_sum(halo + row, POOL_ROWS)
            if level is not None:
                level[pl.ds(halo + row, POOL_ROWS), :] = s
            pos = tile * tm + row + lax.broadcasted_iota(jnp.int32, (POOL_ROWS, 1), 0)
            cnt = jnp.minimum(w, pos + 1).astype(F32)
            pooled = s[:, :group] / cnt - ext_ref[pl.ds(halo + row, POOL_ROWS), own_cols]
            pooled_ref[pl.ds(row, POOL_ROWS), own_cols] = pooled.astype(BF16)
        mixed = jnp.dot(pooled_ref[:, own_cols], pw_ref[g], preferred_element_type=F32) * ps_ref[:, own_cols]
        mix_ref[:, own_cols] = (mixed * _silu(gp_ref[:, own_cols])).astype(BF16)
        if level is not None:
            below, below_cols = level, slice(group, d_pool - g * group)
    mix_ref[:, d_pool:] = (a_ref[...] * _silu(ga_ref[...])).astype(BF16)


def _out_proj_kernel(x_ref, u_ref, halo_ref, gp_ref, ga_ref, a_ref, pw_ref, ps_ref, wo_ref, gf_ref,
                     o_ref, ext_ref, level_refs, pooled_ref, mix_ref, *, tm, tiles_per_seq, d_pool, final_norm):
    tile = pl.program_id(0) % tiles_per_seq
    _prompt_mix(u_ref, halo_ref, gp_ref, ga_ref, a_ref, pw_ref, ps_ref, ext_ref, level_refs, pooled_ref, mix_ref,
                tile=tile, tm=tm, d_pool=d_pool)
    acc = x_ref[...] + jnp.dot(mix_ref[...], wo_ref[...], preferred_element_type=F32)
    if final_norm:
        acc = acc * lax.rsqrt(jnp.mean(acc * acc, axis=-1, keepdims=True) + RMS_EPS) * gf_ref[...]
    o_ref[...] = acc


def _out_proj(x, z, a, pool_w_bf16, pool_scale, w_out_bf16, layer, final_g, *, seq, tm, d_pool, ga_off,
              final_norm):
    m, d = x.shape
    n_grp = len(POOL_WINDOWS)
    halo = n_grp * SUBLANES
    grp = d_pool // n_grp
    d_attn = a.shape[1]
    assert ga_off % d_attn == 0 and d_pool == d_attn and halo >= POOL_MAX - 1 and tm % POOL_ROWS == 0
    return pl.pallas_call(
        functools.partial(_out_proj_kernel, tm=tm, tiles_per_seq=seq // tm, d_pool=d_pool,
                          final_norm=final_norm),
        out_shape=jax.ShapeDtypeStruct((m, d), F32),
        grid=(m // tm,),
        in_specs=[
            pl.BlockSpec((tm, d), lambda i: (i, 0)),
            pl.BlockSpec((tm, d_pool), lambda i: (i, 0)),
            pl.BlockSpec((halo, d_pool), lambda i: (jnp.maximum(i * (tm // halo) - 1, 0), 0)),
            pl.BlockSpec((tm, d_pool), lambda i: (i, 1)),
            pl.BlockSpec((tm, d_attn), lambda i: (i, ga_off // d_attn)),
            pl.BlockSpec((tm, d_attn), lambda i: (i, 0)),
            pl.BlockSpec((None, n_grp, grp, grp), lambda i: (layer, 0, 0, 0)),
            pl.BlockSpec((None, 1, d_pool), lambda i: (layer, 0, 0)),
            pl.BlockSpec((d_pool + d_attn, d), lambda i: (0, 0), pipeline_mode=pl.Buffered(1)),
            pl.BlockSpec((1, d), lambda i: (0, 0)),
        ],
        out_specs=pl.BlockSpec((tm, d), lambda i: (i, 0)),
        scratch_shapes=[pltpu.VMEM((halo + tm, d_pool), F32),
                        [pltpu.VMEM((halo + tm, d_pool - g * grp), F32) for g in range(n_grp - 1)],
                        pltpu.VMEM((tm, d_pool), BF16),
                        pltpu.VMEM((tm, d_pool + d_attn), BF16)],
        compiler_params=_params("parallel"),
        name="out_proj",
    )(x, z, z, z, z, a, pool_w_bf16, pool_scale, w_out_bf16, final_g.reshape(1, d))


def _sample_bias_table(n_heads, l_win):
    slopes = _alibi_slopes(n_heads)
    dist = l_win - np.arange(l_win)
    table = np.empty((n_heads, SUBLANES, l_win), np.float32)
    for p, (window, d) in enumerate(DILATED_PATTERNS):
        valid = (dist % d == 0) & (dist <= window)
        table[:, p, :] = np.where(valid[None, :], -slopes[:, None] * dist[None, :].astype(np.float32), -np.inf)
    table[:, N_PATTERNS:, :] = table[:, :1, :]
    return jnp.asarray(table)


def _sample_mix_kernel(z_ref, st_ref, kt_ref, vt_ref, bias_ref, pw_ref, ps_ref,
                       mix_ref, st_out_ref, ext_ref, a_scr, *, d_pool, d_attn, pos0):
    n_heads = d_attn // HEAD_DIM
    u = z_ref[:, 0:d_pool]
    gp = z_ref[:, d_pool:2 * d_pool]
    ga = z_ref[:, 2 * d_pool + 3 * d_attn:2 * d_pool + 4 * d_attn]
    q_off, k_off, v_off = 2 * d_pool, 2 * d_pool + d_attn, 2 * d_pool + 2 * d_attn

    n_state = POOL_MAX - 1
    ext_ref[pl.ds(0, 1), :] = jnp.zeros((1, d_pool), F32)
    ext_ref[pl.ds(1, n_state), :] = st_ref[...]
    ext_ref[pl.ds(POOL_MAX, 1), :] = u
    pos = jnp.full((1, 1), pos0, jnp.int32)
    group = d_pool // len(POOL_WINDOWS)
    parts = []
    for g, w in enumerate(POOL_WINDOWS):
        cols = slice(g * group, (g + 1) * group)
        pooled = _pooled_window(ext_ref, POOL_MAX, 1, cols, w, pos)
        parts.append(jnp.dot(pooled.astype(BF16), pw_ref[g], preferred_element_type=F32))
    pm = jnp.concatenate(parts, axis=-1) * ps_ref[...]
    st_out_ref[pl.ds(0, n_state - 1), :] = st_ref[pl.ds(1, n_state - 1), :]
    st_out_ref[pl.ds(n_state - 1, 1), :] = u

    is_pattern = lax.broadcasted_iota(jnp.int32, (SUBLANES, 1), 0) < N_PATTERNS
    for h in range(n_heads):
        cols = slice(h * HEAD_DIM, (h + 1) * HEAD_DIM)

        def head_row(off, cols=cols):
            return z_ref[:, off:off + d_attn][:, cols]

        q8 = jnp.broadcast_to(head_row(q_off) * (HEAD_DIM ** -0.5), (SUBLANES, HEAD_DIM)).astype(BF16)
        k_new = head_row(k_off).astype(BF16).astype(F32)
        v_new = head_row(v_off).astype(BF16).astype(F32)
        s = jnp.dot(q8, kt_ref[h].astype(BF16), preferred_element_type=F32) + bias_ref[h]
        s_new = jnp.sum(q8.astype(F32) * k_new, axis=-1, keepdims=True)
        m = jnp.maximum(jnp.max(s, axis=-1, keepdims=True), s_new)
        p = jnp.exp(s - m)
        p_new = jnp.exp(s_new - m)
        den = jnp.sum(p, axis=-1, keepdims=True) + p_new
        o = lax.dot_general(p.astype(BF16), vt_ref[h].astype(BF16), NT_DIMS, preferred_element_type=F32)
        o = (o + p_new.astype(BF16).astype(F32) * v_new) / den
        lse = jnp.where(is_pattern, m + jnp.log(den), -jnp.inf)
        e = jnp.exp(lse - jnp.max(lse, axis=0, keepdims=True))
        a_scr[:, cols] = jnp.sum(e * o, axis=0, keepdims=True) / jnp.sum(e, axis=0, keepdims=True)

    mix_ref[:, :d_pool] = pm * _silu(gp)
    mix_ref[:, d_pool:] = a_scr[...] * _silu(ga)


def _sample_mix(z, state, kt_cache, vt_cache, layer, bias, pool_w_bf16, pool_scale, *, d_pool, d_attn, pos0):
    bd, n_state, _ = state.shape
    n_heads, _, l_win = kt_cache.shape[2:]
    n_in = z.shape[1]
    n_grp = len(POOL_WINDOWS)
    grp = d_pool // n_grp
    cache_spec = pl.BlockSpec((None, None, n_heads, HEAD_DIM, l_win), lambda b: (layer, b, 0, 0, 0))
    return pl.pallas_call(
        functools.partial(_sample_mix_kernel, d_pool=d_pool, d_attn=d_attn, pos0=pos0),
        out_shape=(jax.ShapeDtypeStruct((bd, 1, d_pool + d_attn), F32),
                   jax.ShapeDtypeStruct((bd, n_state, d_pool), F32)),
        grid=(bd,),
        in_specs=[
            pl.BlockSpec((None, 1, n_in), lambda b: (b, 0, 0)),
            pl.BlockSpec((None, n_state, d_pool), lambda b: (b, 0, 0)),
            cache_spec, cache_spec,
            pl.BlockSpec((n_heads, SUBLANES, l_win), lambda b: (0, 0, 0)),
            pl.BlockSpec((None, n_grp, grp, grp), lambda b: (layer, 0, 0, 0)),
            pl.BlockSpec((None, 1, d_pool), lambda b: (layer, 0, 0)),
        ],
        out_specs=(pl.BlockSpec((None, 1, d_pool + d_attn), lambda b: (b, 0, 0)),
                   pl.BlockSpec((None, n_state, d_pool), lambda b: (b, 0, 0))),
        scratch_shapes=[pltpu.VMEM((POOL_MAX + SUBLANES, d_pool), F32),
                        pltpu.VMEM((1, d_attn), F32)],
        compiler_params=_params("parallel"),
        name="sample_mix",
    )(z.reshape(bd, 1, n_in), state, kt_cache, vt_cache, bias, pool_w_bf16, pool_scale)


def _resid_matmul_cast_kernel(x_ref, mix_ref, w_ref, o_ref, wb_ref):
    wb_ref[...] = w_ref[...].astype(BF16)
    o_ref[...] = x_ref[...] + jnp.dot(mix_ref[...].astype(BF16), wb_ref[...], preferred_element_type=F32)


def _resid_matmul_cast(x, mix, w_f32, layer, *, tn):
    m, d = x.shape
    kdim = mix.shape[1]
    return pl.pallas_call(
        _resid_matmul_cast_kernel,
        out_shape=(jax.ShapeDtypeStruct((m, d), F32), jax.ShapeDtypeStruct((kdim, d), BF16)),
        grid=(d // tn,),
        in_specs=[pl.BlockSpec((m, tn), lambda j: (0, j)),
                  pl.BlockSpec((m, kdim), lambda j: (0, 0)),
                  pl.BlockSpec((None, kdim, tn), lambda j: (layer, 0, j))],
        out_specs=(pl.BlockSpec((m, tn), lambda j: (0, j)),
                   pl.BlockSpec((kdim, tn), lambda j: (0, j))),
        compiler_params=_params("parallel"),
        name="sample_out_proj",
    )(x, mix, w_f32)


def kernel(x_prompt, x_sample, state_pool, cache_k_win, cache_v_win, norm_g, w_in, pool_w, pool_scale,
           w_out, final_norm_g):
    batch, seq, d_model = x_prompt.shape
    bd, dec_seq, _ = x_sample.shape
    depth = norm_g.shape[0]
    d_pool = pool_scale.shape[1]
    d_attn = w_out.shape[1] - d_pool
    n_heads = d_attn // HEAD_DIM
    l_win = cache_k_win.shape[2]
    assert dec_seq == 1 and l_win == WIN_MAX and PAST_LEN + 1 >= POOL_MAX
    assert seq % (BAND * max(d for _, d in DILATED_PATTERNS)) == 0 and seq >= WIN_MAX
    q_off, k_off, v_off, ga_off = (2 * d_pool, 2 * d_pool + d_attn, 2 * d_pool + 2 * d_attn,
                                   2 * d_pool + 3 * d_attn)

    tiles = _tile_plan(seq, d_model, d_attn)
    pool_w_b = pool_w.astype(BF16)
    gains = norm_g.reshape(depth, 1, d_model)
    scales = pool_scale.reshape(depth, 1, d_pool)
    bias_p = _prompt_bias_table(n_heads)
    bias_s = _sample_bias_table(n_heads, l_win)

    kt_cache = cache_k_win.transpose(0, 1, 3, 4, 2)
    vt_cache = cache_v_win.transpose(0, 1, 3, 4, 2)
    xs = x_sample.reshape(bd, d_model)
    pool_s, k_s, v_s, w_in_b, w_out_b = [], [], [], [], []
    for l in range(depth):
        z, wb = _in_proj_cast(xs, gains, w_in, l, tn=tiles["sample_cols"])
        w_in_b.append(wb)
        mix, st = _sample_mix(z, state_pool[l], kt_cache, vt_cache, l, bias_s, pool_w_b, scales,
                              d_pool=d_pool, d_attn=d_attn, pos0=PAST_LEN)
        xs, wb = _resid_matmul_cast(xs, mix.reshape(bd, -1), w_out, l, tn=tiles["sample_cols"])
        w_out_b.append(wb)
        pool_s.append(st)
        k_s.append(z[:, k_off:k_off + d_attn].reshape(bd, 1, n_heads, HEAD_DIM))
        v_s.append(z[:, v_off:v_off + d_attn].reshape(bd, 1, n_heads, HEAD_DIM))
    y_sample = _rms_norm(xs, final_norm_g, tm=bd).reshape(bd, 1, d_model)

    keep = min(WIN_MAX, seq)
    xp = x_prompt.reshape(batch * seq, d_model)
    pool_p, windows = [], None
    for l in range(depth):
        z, *windows = _in_proj_window(xp, gains, w_in_b[l], l, depth, windows, seq=seq, keep=keep,
                                      k_off=k_off, v_off=v_off, d_attn=d_attn, tm=tiles["in_proj_rows"])
        a = _attn_prompt(z, bias_p, batch=batch, seq=seq, q_off=q_off, k_off=k_off, v_off=v_off, d_attn=d_attn)
        xp = _out_proj(xp, z, a, pool_w_b, scales, w_out_b[l], l, final_norm_g, seq=seq, tm=tiles["out_proj_rows"],
                       d_pool=d_pool, ga_off=ga_off, final_norm=(l == depth - 1))
        pool_p.append(z.reshape(batch, seq, -1)[:, seq - (POOL_MAX - 1):, :d_pool])
    y_prompt = xp.reshape(batch, seq, d_model)

    def from_slabs(slabs):
        return slabs.reshape(depth, batch, n_heads, HEAD_DIM, keep).transpose(0, 1, 4, 2, 3)

    return (y_prompt, y_sample,
            jnp.stack(pool_p), from_slabs(windows[0]), from_slabs(windows[1]),
            jnp.stack(pool_s), jnp.stack(k_s), jnp.stack(v_s))
```

```python
import functools

import numpy as np
import jax
import jax.numpy as jnp
from jax import lax
from jax.experimental import pallas as pl
from jax.experimental.pallas import tpu as pltpu

HEAD_DIM = 64
LANES = 128
SUBLANES = 8
HEADS_PER_TILE = LANES // HEAD_DIM
POOL_WINDOWS = (2, 4, 8, 16)
POOL_MAX = max(POOL_WINDOWS)
DILATED_PATTERNS = ((128, 1), (512, 4), (2048, 16))
N_PATTERNS = len(DILATED_PATTERNS)
BAND = 128
WIN_MAX = max(w for w, _ in DILATED_PATTERNS)
PAST_LEN = 16384
RMS_EPS = 1e-6
VMEM_LIMIT_BYTES = 56 * 2**20

F32 = jnp.float32
BF16 = jnp.bfloat16
NT_DIMS = (((1,), (1,)), ((), ()))


def _alibi_slopes(n_heads):
    return (2.0 ** (-8.0 * np.arange(1, n_heads + 1) / n_heads)).astype(np.float32)


def _silu(x):
    return x * (1.0 / (1.0 + jnp.exp(-x)))


def _params(*sem):
    return pltpu.CompilerParams(dimension_semantics=sem, vmem_limit_bytes=VMEM_LIMIT_BYTES)


def _tile_plan(seq, d_model, d_attn):
    in_proj_rows = min(1024, seq)
    out_proj_rows = min(512, seq)
    in_proj_bytes = (2 * 4 + 2) * in_proj_rows * d_model + 2 * 2 * d_model * d_attn + 3 * 2 * 4 * in_proj_rows * d_attn
    out_proj_bytes = 2 * d_model * d_model + (2 * 2 + 4 * 2 // 2) * 4 * out_proj_rows * d_model
    assert max(in_proj_bytes, out_proj_bytes) <= VMEM_LIMIT_BYTES
    return dict(in_proj_rows=in_proj_rows, out_proj_rows=out_proj_rows, sample_cols=d_attn)


NORM_ROWS = 256


def _normed_bf16(x_ref, g_ref, h_ref):
    x = x_ref[...]
    y = x * lax.rsqrt(jnp.mean(x * x, axis=-1, keepdims=True) + RMS_EPS)
    h_ref[...] = (y * g_ref[...]).astype(BF16)


def _in_proj_cast_kernel(x_ref, g_ref, w_ref, z_ref, wb_ref, h_ref):
    @pl.when(pl.program_id(0) == 0)
    def _():
        _normed_bf16(x_ref, g_ref, h_ref)

    wb_ref[...] = w_ref[...].astype(BF16)
    z_ref[...] = jnp.dot(h_ref[...], wb_ref[...], preferred_element_type=F32)


def _in_proj_window_kernel(x_ref, g_ref, w_ref, *rest, tiles_per_seq, first_win_tile, k_tile, v_tile,
                           zero_fill):
    z_ref, kt_ref, vt_ref, h_ref = rest[-4:]
    j = pl.program_id(1)
    tm = x_ref.shape[0]

    @pl.when(j == 0)
    def _():
        for row in range(0, tm, NORM_ROWS):
            rows = pl.ds(row, NORM_ROWS)
            x = x_ref[rows, :]
            y = x * lax.rsqrt(jnp.mean(x * x, axis=-1, keepdims=True) + RMS_EPS)
            h_ref[rows, :] = (y * g_ref[...]).astype(BF16)
            z_ref[rows, :] = jnp.dot(h_ref[rows, :], w_ref[...], preferred_element_type=F32)

    in_window = pl.program_id(0) % tiles_per_seq >= first_win_tile
    assert k_tile > 0 and v_tile > 0
    to_window = jnp.logical_and(in_window, jnp.logical_or(j == k_tile, j == v_tile))

    @pl.when(jnp.logical_and(j > 0, jnp.logical_not(to_window)))
    def _():
        z_ref[...] = jnp.dot(h_ref[...], w_ref[...], preferred_element_type=F32)

    for tile_j, win_ref in ((k_tile, kt_ref), (v_tile, vt_ref)):
        @pl.when(jnp.logical_and(in_window, j == tile_j))
        def _(win_ref=win_ref):
            for row in range(0, tm, NORM_ROWS):
                rows = pl.ds(row, NORM_ROWS)
                zc = jnp.dot(h_ref[rows, :], w_ref[...], preferred_element_type=F32)
                z_ref[rows, :] = zc
                win_ref[:, rows] = zc.T

    if zero_fill:
        @pl.when(jnp.logical_not(in_window))
        def _():
            kt_ref[...] = jnp.zeros_like(kt_ref)
            vt_ref[...] = jnp.zeros_like(vt_ref)


def _in_proj_cast(x, g, w_f32, layer, *, tn):
    m, d = x.shape
    n = w_f32.shape[2]
    return pl.pallas_call(
        _in_proj_cast_kernel,
        out_shape=(jax.ShapeDtypeStruct((m, n), F32), jax.ShapeDtypeStruct((d, n), BF16)),
        grid=(n // tn,),
        in_specs=[
            pl.BlockSpec((m, d), lambda j: (0, 0)),
            pl.BlockSpec((None, 1, d), lambda j: (layer, 0, 0)),
            pl.BlockSpec((None, d, tn), lambda j: (layer, 0, j)),
        ],
        out_specs=(pl.BlockSpec((m, tn), lambda j: (0, j)),
                   pl.BlockSpec((d, tn), lambda j: (0, j))),
        scratch_shapes=[pltpu.VMEM((m, d), BF16)],
        compiler_params=_params("arbitrary"),
        name="in_proj_sample",
    )(x, g, w_f32)


def _in_proj_window(x, g, w_bf16, layer, depth, windows, *, seq, keep, k_off, v_off, d_attn, tm):
    m, d = x.shape
    n = w_bf16.shape[1]
    tn = d_attn
    assert k_off % tn == 0 and v_off % tn == 0 and (seq - keep) % tm == 0 and seq % tm == 0
    batch = m // seq
    tiles_per_seq = seq // tm
    first_win_tile = (seq - keep) // tm
    win_tiles = tiles_per_seq - first_win_tile
    n_col = n // tn
    create = windows is None
    zero_fill = create and depth > 1
    n_slots = batch * first_win_tile * n_col
    n_zero = (depth - 1) * batch * win_tiles
    assert (layer == 0) == create and (not zero_fill or n_slots >= n_zero)

    def win_map(i, j):
        b, t = i // tiles_per_seq, i % tiles_per_seq
        own = (layer, b, 0, jnp.maximum(t - first_win_tile, 0))
        if not zero_fill:
            return own
        zero_block = ((b * first_win_tile + t) * n_col + j) * n_zero // n_slots
        other = (1 + zero_block // (batch * win_tiles), (zero_block // win_tiles) % batch, 0,
                 zero_block % win_tiles)
        return tuple(jnp.where(t >= first_win_tile, o, z) for o, z in zip(own, other))

    win = jax.ShapeDtypeStruct((depth, batch, d_attn, keep), F32)
    win_spec = pl.BlockSpec((None, None, d_attn, tm), win_map)
    extra_specs = [] if create else [pl.BlockSpec(memory_space=pl.ANY)] * 2
    return pl.pallas_call(
        functools.partial(_in_proj_window_kernel, tiles_per_seq=tiles_per_seq, first_win_tile=first_win_tile,
                          k_tile=k_off // tn, v_tile=v_off // tn, zero_fill=zero_fill),
        out_shape=(jax.ShapeDtypeStruct((m, n), F32), win, win),
        grid=(m // tm, n_col),
        in_specs=[
            pl.BlockSpec((tm, d), lambda i, j: (i, 0)),
            pl.BlockSpec((None, 1, d), lambda i, j: (layer, 0, 0)),
            pl.BlockSpec((d, tn), lambda i, j: (0, j)),
            *extra_specs,
        ],
        out_specs=(pl.BlockSpec((tm, tn), lambda i, j: (i, j)), win_spec, win_spec),
        scratch_shapes=[pltpu.VMEM((tm, d), BF16)],
        input_output_aliases={} if create else {3: 1, 4: 2},
        compiler_params=_params("arbitrary", "arbitrary"),
        name="in_proj_prompt",
    )(x, g, w_bf16, *(() if create else windows))


def _rms_norm_kernel(x_ref, g_ref, o_ref):
    x = x_ref[...]
    y = x * lax.rsqrt(jnp.mean(x * x, axis=-1, keepdims=True) + RMS_EPS)
    o_ref[...] = y * g_ref[...]


def _rms_norm(x, g, *, tm):
    m, d = x.shape
    return pl.pallas_call(
        _rms_norm_kernel,
        out_shape=jax.ShapeDtypeStruct((m, d), F32),
        grid=(m // tm,),
        in_specs=[pl.BlockSpec((tm, d), lambda i: (i, 0)),
                  pl.BlockSpec((1, d), lambda i: (0, 0))],
        out_specs=pl.BlockSpec((tm, d), lambda i: (i, 0)),
        compiler_params=_params("parallel"),
        name="final_norm",
    )(x, g.reshape(1, d))


def _prompt_bias_table(n_heads):
    slopes = _alibi_slopes(n_heads)
    qi = np.arange(BAND)[:, None] + BAND
    kj = np.arange(2 * BAND)[None, :]
    dist = qi - kj
    valid = (dist >= 0) & (dist <= BAND)
    valid_first = valid & (kj >= BAND)
    table = np.empty((n_heads, N_PATTERNS, 2, BAND, 2 * BAND), np.float32)
    for h in range(n_heads):
        for p, (_, d) in enumerate(DILATED_PATTERNS):
            bias = -slopes[h] * (dist * d).astype(np.float32)
            table[h, p, 0] = np.where(valid, bias, -np.inf)
            table[h, p, 1] = np.where(valid_first, bias, -np.inf)
    table = table.reshape(n_heads // HEADS_PER_TILE, HEADS_PER_TILE, N_PATTERNS, 2, BAND, 2 * BAND)
    table = table.transpose(0, 2, 3, 1, 4, 5)
    return jnp.asarray(table.reshape(n_heads // HEADS_PER_TILE, N_PATTERNS, 2, HEADS_PER_TILE * BAND, 2 * BAND))


ATTN_GROUP = 4
SOFTMAX_ROWS = 32


PREP_ROWS = 256


def _attn_prompt_kernel(q_ref, k_ref, v_ref, bias_ref, a_ref,
                        o_scr, m_scr, d_scr, qd_scrs, kd_scrs, vd_scrs,
                        group_bufs, *, seq):
    lane = lax.broadcasted_iota(jnp.int32, (BAND, LANES), 1)
    head0 = lane < HEAD_DIM
    stacked = HEADS_PER_TILE * BAND
    scale = HEAD_DIM ** -0.5
    srcs = (q_ref, k_ref, v_ref)
    assert DILATED_PATTERNS[1][1] ** 2 == DILATED_PATTERNS[2][1] and DILATED_PATTERNS[0][1] == 1
    d_mid, d_big = DILATED_PATTERNS[1][1], DILATED_PATTERNS[2][1]

    def put(ti, pi, sub, row, val):
        n_sub = seq // DILATED_PATTERNS[pi][1]
        if ti == 0:
            qd_scrs[pi][pl.ds(sub * n_sub + row, PREP_ROWS), :] = (val * scale).astype(BF16)
        else:
            dst = (kd_scrs, vd_scrs)[ti - 1][pi]
            dst[pl.ds(sub * (BAND + n_sub) + BAND + row, PREP_ROWS), :] = val.astype(BF16)

    for pi, (_, d) in enumerate(DILATED_PATTERNS):
        n_sub = seq // d
        for dst in (kd_scrs[pi], vd_scrs[pi]):
            for r in range(d):
                dst[pl.ds(r * (BAND + n_sub), BAND), :] = jnp.zeros((BAND, LANES), BF16)

    def prep_dense(c, carry):
        row = pl.multiple_of(c * PREP_ROWS, PREP_ROWS)
        for ti in range(3):
            put(ti, 0, 0, row, srcs[ti][pl.ds(row, PREP_ROWS), :])
        return carry

    lax.fori_loop(0, seq // PREP_ROWS, prep_dense, 0)

    n_mid = seq // d_mid

    def prep_mid(c, carry):
        row = pl.multiple_of(c * PREP_ROWS, PREP_ROWS)
        for ti in range(3):
            for r in range(d_mid):
                val = srcs[ti][pl.ds(r + d_mid * row, PREP_ROWS, stride=d_mid), :]
                o_scr[ti, pl.ds(r * n_mid + row, PREP_ROWS), :] = val
                put(ti, 1, r, row, val)
        return carry

    lax.fori_loop(0, n_mid // PREP_ROWS, prep_mid, 0)

    n_big = seq // d_big
    for c in range(n_big // PREP_ROWS):
        for ti in range(3):
            for r in range(d_mid):
                for a in range(d_mid):
                    val = o_scr[ti, pl.ds(r * n_mid + a + d_mid * c * PREP_ROWS, PREP_ROWS, stride=d_mid), :]
                    put(ti, 2, r + d_mid * a, c * PREP_ROWS, val)

    def aligned(row):
        return row if isinstance(row, int) else pl.multiple_of(row, BAND)

    stages = []
    for pi, (_, d) in enumerate(DILATED_PATTERNS):
        nblk = seq // d // BAND
        n_sub = seq // d
        qd, kd, vd = qd_scrs[pi], kd_scrs[pi], vd_scrs[pi]

        def units_of(grp, d=d, nblk=nblk, n_sub=n_sub):
            units = []
            for g in range(ATTN_GROUP):
                idx = grp * ATTN_GROUP + g
                r = idx // nblk
                blk = idx % nblk
                q_row = aligned(r * n_sub + blk * BAND)
                k_row = aligned(r * (BAND + n_sub) + blk * BAND)
                if d == d_big:
                    out_rows = pl.ds((r % d_mid) * n_mid + r // d_mid + d_mid * BAND * blk, BAND, stride=d_mid)
                else:
                    out_rows = pl.ds(q_row, BAND)
                first = int(blk == 0) if isinstance(blk, int) else jnp.asarray(blk == 0, jnp.int32)
                units.append((q_row, k_row, out_rows, first))
            return units

        def scores(units, bufs, pi=pi, qd=qd, kd=kd):
            s_buf, _, mg_buf, _ = bufs
            for g, (q_row, k_row, _, first) in enumerate(units):
                q = qd[pl.ds(q_row, BAND), :]
                zero = jnp.zeros_like(q)
                q2 = jnp.concatenate([jnp.where(head0, q, zero), jnp.where(head0, zero, q)], axis=0)
                s = lax.dot_general(q2, kd[pl.ds(k_row, 2 * BAND), :], NT_DIMS,
                                    preferred_element_type=F32) + bias_ref[pi, first]
                s_buf[g] = s
                mg_buf[g] = jnp.broadcast_to(jnp.max(s, axis=-1, keepdims=True), (stacked, LANES))

        def softmax_values(units, bufs, pi=pi, vd=vd):
            s_buf, p_buf, mg_buf, dg_buf = bufs
            for g in range(ATTN_GROUP):
                for c in range(stacked // SOFTMAX_ROWS):
                    sl = pl.ds(c * SOFTMAX_ROWS, SOFTMAX_ROWS)
                    m = mg_buf[g, sl, :]
                    p = jnp.exp(s_buf[g, sl, :] - jnp.concatenate([m, m], axis=-1))
                    p_buf[g, sl, :] = p.astype(BF16)
                    dg_buf[g, sl, :] = jnp.broadcast_to(jnp.sum(p, axis=-1, keepdims=True), (SOFTMAX_ROWS, LANES))
            for g, (_, _, out_rows, _) in enumerate(units):
                m_scr[pi, out_rows, :] = jnp.where(head0, mg_buf[g, pl.ds(0, BAND), :], mg_buf[g, pl.ds(BAND, BAND), :])
                d_scr[pi, out_rows, :] = jnp.where(head0, dg_buf[g, pl.ds(0, BAND), :], dg_buf[g, pl.ds(BAND, BAND), :])
            for g, (_, k_row, out_rows, _) in enumerate(units):
                o2 = jnp.dot(p_buf[g], vd[pl.ds(k_row, 2 * BAND), :], preferred_element_type=F32)
                o_scr[pi, out_rows, :] = jnp.where(head0, o2[:BAND], o2[BAND:])

        stages.append((units_of, scores, softmax_values, d * nblk // ATTN_GROUP))

    def pair(j, units_of, scores, softmax_values, score_ahead):
        even, odd = units_of(2 * j), units_of(2 * j + 1)
        scores(odd, group_bufs[1])
        softmax_values(even, group_bufs[0])
        score_ahead()
        softmax_values(odd, group_bufs[1])

    stages[0][1](stages[0][0](0), group_bufs[0])
    for pi, (units_of, scores, softmax_values, n_groups) in enumerate(stages):
        def body(j, carry, units_of=units_of, scores=scores, softmax_values=softmax_values):
            pair(j, units_of, scores, softmax_values, lambda: scores(units_of(2 * j + 2), group_bufs[0]))
            return carry

        lax.fori_loop(0, n_groups // 2 - 1, body, 0)
        if pi + 1 < len(stages):
            next_units_of, next_scores = stages[pi + 1][:2]
            score_ahead = functools.partial(next_scores, next_units_of(0), group_bufs[0])
        else:
            score_ahead = lambda: None
        pair(n_groups // 2 - 1, units_of, scores, softmax_values, score_ahead)

    chunk = 2 * BAND

    def combine(c, carry):
        start = pl.multiple_of(c * chunk, chunk)
        sl = pl.ds(start, chunk)
        natural = pl.ds(start // n_mid + d_mid * (start % n_mid), chunk, stride=d_mid)
        ms = [m_scr[0, natural, :], m_scr[1, sl, :], m_scr[2, sl, :]]
        m = jnp.maximum(jnp.maximum(ms[0], ms[1]), ms[2])
        es = [jnp.exp(mi - m) for mi in ms]
        num = es[0] * o_scr[0, natural, :] + es[1] * o_scr[1, sl, :] + es[2] * o_scr[2, sl, :]
        den = es[0] * d_scr[0, natural, :] + es[1] * d_scr[1, sl, :] + es[2] * d_scr[2, sl, :]
        a_ref[natural, :] = num / den
        return carry

    lax.fori_loop(0, seq // chunk, combine, 0)


def _attn_prompt(z, bias, *, batch, seq, q_off, k_off, v_off, d_attn):
    n_pairs = d_attn // LANES
    stacked = HEADS_PER_TILE * BAND

    def col(off):
        return pl.BlockSpec((seq, LANES), lambda b, hp: (b, off // LANES + hp))

    return pl.pallas_call(
        functools.partial(_attn_prompt_kernel, seq=seq),
        out_shape=jax.ShapeDtypeStruct((batch * seq, d_attn), F32),
        grid=(batch, n_pairs),
        in_specs=[
            col(q_off), col(k_off), col(v_off),
            pl.BlockSpec((None, N_PATTERNS, 2, stacked, 2 * BAND), lambda b, hp: (hp, 0, 0, 0, 0)),
        ],
        out_specs=pl.BlockSpec((seq, LANES), lambda b, hp: (b, hp)),
        scratch_shapes=[pltpu.VMEM((N_PATTERNS, seq, LANES), F32),
                        pltpu.VMEM((N_PATTERNS, seq, LANES), F32),
                        pltpu.VMEM((N_PATTERNS, seq, LANES), F32),
                        [pltpu.VMEM((seq, LANES), BF16) for _ in DILATED_PATTERNS],
                        [pltpu.VMEM((seq + d * BAND, LANES), BF16) for _, d in DILATED_PATTERNS],
                        [pltpu.VMEM((seq + d * BAND, LANES), BF16) for _, d in DILATED_PATTERNS],
                        [[pltpu.VMEM((ATTN_GROUP, stacked, 2 * BAND), F32),
                          pltpu.VMEM((ATTN_GROUP, stacked, 2 * BAND), BF16),
                          pltpu.VMEM((ATTN_GROUP, stacked, LANES), F32),
                          pltpu.VMEM((ATTN_GROUP, stacked, LANES), F32)]
                         for _ in range(2)]],
        compiler_params=_params("parallel", "parallel"),
        name="attn_prompt",
    )(z, z, z, bias)


def _pooled_window(ext_ref, row0, n_rows, cols, w, pos):
    cur = ext_ref[pl.ds(row0, n_rows), cols]
    acc = cur
    for i in range(1, w):
        acc = acc + ext_ref[pl.ds(row0 - i, n_rows), cols]
    return acc / jnp.minimum(w, pos + 1).astype(F32) - cur


POOL_ROWS = 64


def _prompt_mix(u_ref, halo_ref, gp_ref, ga_ref, a_ref, pw_ref, ps_ref, ext_ref, level_refs, pooled_ref, mix_ref,
                *, tile, tm, d_pool):
    halo = halo_ref.shape[0]
    n_grp = len(POOL_WINDOWS)
    group = d_pool // n_grp
    assert all(w == 2 << g for g, w in enumerate(POOL_WINDOWS)) and halo >= (n_grp) * SUBLANES
    ext_ref[pl.ds(0, halo), :] = jnp.where(tile == 0, 0.0, halo_ref[...])
    ext_ref[pl.ds(halo, tm), :] = u_ref[...]
    below, below_cols = ext_ref, slice(0, d_pool)
    for g, w in enumerate(POOL_WINDOWS):
        shift = w // 2
        level = level_refs[g] if g + 1 < n_grp else None
        own_cols = slice(g * group, (g + 1) * group)

        def window_sum(row, n, below=below, below_cols=below_cols, shift=shift):
            return below[pl.ds(row, n), below_cols] + below[pl.ds(row - shift, n), below_cols]

        first = (g + 1) * SUBLANES
        if level is not None and first < halo:
            level[pl.ds(first, halo - first), :] = window_sum(first, halo - first)
        for row in range(0, tm, POOL_ROWS):
            s = window_sum(halo + row, POOL_ROWS)
            if level is not None:
                level[pl.ds(halo + row, POOL_ROWS), :] = s
            pos = tile * tm + row + lax.broadcasted_iota(jnp.int32, (POOL_ROWS, 1), 0)
            cnt = jnp.minimum(w, pos + 1).astype(F32)
            pooled = s[:, :group] / cnt - ext_ref[pl.ds(halo + row, POOL_ROWS), own_cols]
            pooled_ref[pl.ds(row, POOL_ROWS), own_cols] = pooled.astype(BF16)
        mixed = jnp.dot(pooled_ref[:, own_cols], pw_ref[g], preferred_element_type=F32) * ps_ref[:, own_cols]
        mix_ref[:, own_cols] = (mixed * _silu(gp_ref[:, own_cols])).astype(BF16)
        if level is not None:
            below, below_cols = level, slice(group, d_pool - g * group)
    mix_ref[:, d_pool:] = (a_ref[...] * _silu(ga_ref[...])).astype(BF16)


def _out_proj_kernel(x_ref, u_ref, halo_ref, gp_ref, ga_ref, a_ref, pw_ref, ps_ref, wo_ref, gf_ref,
                     o_ref, ext_ref, level_refs, pooled_ref, mix_ref, *, tm, tiles_per_seq, d_pool, final_norm):
    tile = pl.program_id(0) % tiles_per_seq
    _prompt_mix(u_ref, halo_ref, gp_ref, ga_ref, a_ref, pw_ref, ps_ref, ext_ref, level_refs, pooled_ref, mix_ref,
                tile=tile, tm=tm, d_pool=d_pool)
    acc = x_ref[...] + jnp.dot(mix_ref[...], wo_ref[...], preferred_element_type=F32)
    if final_norm:
        acc = acc * lax.rsqrt(jnp.mean(acc * acc, axis=-1, keepdims=True) + RMS_EPS) * gf_ref[...]
    o_ref[...] = acc


def _out_proj(x, z, a, pool_w_bf16, pool_scale, w_out_bf16, layer, final_g, *, seq, tm, d_pool, ga_off,
              final_norm):
    m, d = x.shape
    n_grp = len(POOL_WINDOWS)
    halo = n_grp * SUBLANES
    grp = d_pool // n_grp
    d_attn = a.shape[1]
    assert ga_off % d_attn == 0 and d_pool == d_attn and halo >= POOL_MAX - 1 and tm % POOL_ROWS == 0
    return pl.pallas_call(
        functools.partial(_out_proj_kernel, tm=tm, tiles_per_seq=seq // tm, d_pool=d_pool,
                          final_norm=final_norm),
        out_shape=jax.ShapeDtypeStruct((m, d), F32),
        grid=(m // tm,),
        in_specs=[
            pl.BlockSpec((tm, d), lambda i: (i, 0)),
            pl.BlockSpec((tm, d_pool), lambda i: (i, 0)),
            pl.BlockSpec((halo, d_pool), lambda i: (jnp.maximum(i * (tm // halo) - 1, 0), 0)),
            pl.BlockSpec((tm, d_pool), lambda i: (i, 1)),
            pl.BlockSpec((tm, d_attn), lambda i: (i, ga_off // d_attn)),
            pl.BlockSpec((tm, d_attn), lambda i: (i, 0)),
            pl.BlockSpec((None, n_grp, grp, grp), lambda i: (layer, 0, 0, 0)),
            pl.BlockSpec((None, 1, d_pool), lambda i: (layer, 0, 0)),
            pl.BlockSpec((d_pool + d_attn, d), lambda i: (0, 0), pipeline_mode=pl.Buffered(1)),
            pl.BlockSpec((1, d), lambda i: (0, 0)),
        ],
        out_specs=pl.BlockSpec((tm, d), lambda i: (i, 0)),
        scratch_shapes=[pltpu.VMEM((halo + tm, d_pool), F32),
                        [pltpu.VMEM((halo + tm, d_pool - g * grp), F32) for g in range(n_grp - 1)],
                        pltpu.VMEM((tm, d_pool), BF16),
                        pltpu.VMEM((tm, d_pool + d_attn), BF16)],
        compiler_params=_params("parallel"),
        name="out_proj",
    )(x, z, z, z, z, a, pool_w_bf16, pool_scale, w_out_bf16, final_g.reshape(1, d))


def _sample_bias_table(n_heads, l_win):
    slopes = _alibi_slopes(n_heads)
    dist = l_win - np.arange(l_win)
    table = np.empty((n_heads, SUBLANES, l_win), np.float32)
    for p, (window, d) in enumerate(DILATED_PATTERNS):
        valid = (dist % d == 0) & (dist <= window)
        table[:, p, :] = np.where(valid[None, :], -slopes[:, None] * dist[None, :].astype(np.float32), -np.inf)
    table[:, N_PATTERNS:, :] = table[:, :1, :]
    return jnp.asarray(table)


SAMPLE_HEADS = 4


def _sample_mix_kernel(z_ref, st_ref, kt_hbm, vt_hbm, bias_ref, pw_ref, ps_ref,
                       mix_ref, st_out_ref, ext_ref, a_scr, k_buf, v_buf, sems, *, d_pool, d_attn, pos0, layer):
    n_heads = d_attn // HEAD_DIM
    n_chunks = n_heads // SAMPLE_HEADS
    assert n_chunks % 2 == 0
    b = pl.program_id(0)
    n_seq = pl.num_programs(0)

    def copies(seq, chunk, slot):
        heads = pl.ds(chunk * SAMPLE_HEADS, SAMPLE_HEADS)
        return (pltpu.make_async_copy(kt_hbm.at[layer, seq, heads], k_buf.at[slot], sems.at[0, slot]),
                pltpu.make_async_copy(vt_hbm.at[layer, seq, heads], v_buf.at[slot], sems.at[1, slot]))

    def start(seq, chunk, slot):
        for c in copies(seq, chunk, slot):
            c.start()

    @pl.when(b == 0)
    def _():
        start(b, 0, 0)
    u = z_ref[:, 0:d_pool]
    gp = z_ref[:, d_pool:2 * d_pool]
    ga = z_ref[:, 2 * d_pool + 3 * d_attn:2 * d_pool + 4 * d_attn]
    q_off, k_off, v_off = 2 * d_pool, 2 * d_pool + d_attn, 2 * d_pool + 2 * d_attn

    n_state = POOL_MAX - 1
    ext_ref[pl.ds(0, 1), :] = jnp.zeros((1, d_pool), F32)
    ext_ref[pl.ds(1, n_state), :] = st_ref[...]
    ext_ref[pl.ds(POOL_MAX, 1), :] = u
    pos = jnp.full((1, 1), pos0, jnp.int32)
    group = d_pool // len(POOL_WINDOWS)
    parts = []
    for g, w in enumerate(POOL_WINDOWS):
        cols = slice(g * group, (g + 1) * group)
        pooled = _pooled_window(ext_ref, POOL_MAX, 1, cols, w, pos)
        parts.append(jnp.dot(pooled.astype(BF16), pw_ref[g], preferred_element_type=F32))
    pm = jnp.concatenate(parts, axis=-1) * ps_ref[...]
    st_out_ref[pl.ds(0, n_state - 1), :] = st_ref[pl.ds(1, n_state - 1), :]
    st_out_ref[pl.ds(n_state - 1, 1), :] = u

    is_pattern = lax.broadcasted_iota(jnp.int32, (SUBLANES, 1), 0) < N_PATTERNS
    for h in range(n_heads):
        chunk, in_chunk = divmod(h, SAMPLE_HEADS)
        slot = chunk % 2
        if in_chunk == 0:
            if chunk + 1 < n_chunks:
                start(b, chunk + 1, 1 - slot)
            else:
                @pl.when(b + 1 < n_seq)
                def _():
                    start(b + 1, 0, 1 - slot)
            for c in copies(b, chunk, slot):
                c.wait()
        kt_ref, vt_ref = k_buf.at[slot], v_buf.at[slot]
        cols = slice(h * HEAD_DIM, (h + 1) * HEAD_DIM)

        def head_row(off, cols=cols):
            return z_ref[:, off:off + d_attn][:, cols]

        q8 = jnp.broadcast_to(head_row(q_off) * (HEAD_DIM ** -0.5), (SUBLANES, HEAD_DIM)).astype(BF16)
        k_new = head_row(k_off).astype(BF16).astype(F32)
        v_new = head_row(v_off).astype(BF16).astype(F32)
        s = jnp.dot(q8, kt_ref[in_chunk].astype(BF16), preferred_element_type=F32) + bias_ref[h]
        s_new = jnp.sum(q8.astype(F32) * k_new, axis=-1, keepdims=True)
        m = jnp.maximum(jnp.max(s, axis=-1, keepdims=True), s_new)
        p = jnp.exp(s - m)
        p_new = jnp.exp(s_new - m)
        den = jnp.sum(p, axis=-1, keepdims=True) + p_new
        o = lax.dot_general(p.astype(BF16), vt_ref[in_chunk].astype(BF16), NT_DIMS, preferred_element_type=F32)
        o = (o + p_new.astype(BF16).astype(F32) * v_new) / den
        lse = jnp.where(is_pattern, m + jnp.log(den), -jnp.inf)
        e = jnp.exp(lse - jnp.max(lse, axis=0, keepdims=True))
        a_scr[:, cols] = jnp.sum(e * o, axis=0, keepdims=True) / jnp.sum(e, axis=0, keepdims=True)

    mix_ref[:, :d_pool] = pm * _silu(gp)
    mix_ref[:, d_pool:] = a_scr[...] * _silu(ga)


def _sample_mix(z, state, kt_cache, vt_cache, layer, bias, pool_w_bf16, pool_scale, *, d_pool, d_attn, pos0):
    bd, n_state, _ = state.shape
    n_heads, _, l_win = kt_cache.shape[2:]
    n_in = z.shape[1]
    n_grp = len(POOL_WINDOWS)
    grp = d_pool // n_grp
    cache_spec = pl.BlockSpec(memory_space=pl.ANY)
    chunk_buf = pltpu.VMEM((2, SAMPLE_HEADS, HEAD_DIM, l_win), F32)
    return pl.pallas_call(
        functools.partial(_sample_mix_kernel, d_pool=d_pool, d_attn=d_attn, pos0=pos0, layer=layer),
        out_shape=(jax.ShapeDtypeStruct((bd, 1, d_pool + d_attn), F32),
                   jax.ShapeDtypeStruct((bd, n_state, d_pool), F32)),
        grid=(bd,),
        in_specs=[
            pl.BlockSpec((None, 1, n_in), lambda b: (b, 0, 0)),
            pl.BlockSpec((None, n_state, d_pool), lambda b: (b, 0, 0)),
            cache_spec, cache_spec,
            pl.BlockSpec((n_heads, SUBLANES, l_win), lambda b: (0, 0, 0)),
            pl.BlockSpec((None, n_grp, grp, grp), lambda b: (layer, 0, 0, 0)),
            pl.BlockSpec((None, 1, d_pool), lambda b: (layer, 0, 0)),
        ],
        out_specs=(pl.BlockSpec((None, 1, d_pool + d_attn), lambda b: (b, 0, 0)),
                   pl.BlockSpec((None, n_state, d_pool), lambda b: (b, 0, 0))),
        scratch_shapes=[pltpu.VMEM((POOL_MAX + SUBLANES, d_pool), F32),
                        pltpu.VMEM((1, d_attn), F32),
                        chunk_buf, chunk_buf,
                        pltpu.SemaphoreType.DMA((2, 2))],
        compiler_params=_params("arbitrary"),
        name="sample_mix",
    )(z.reshape(bd, 1, n_in), state, kt_cache, vt_cache, bias, pool_w_bf16, pool_scale)


def _resid_matmul_cast_kernel(x_ref, mix_ref, w_ref, o_ref, wb_ref):
    wb_ref[...] = w_ref[...].astype(BF16)
    o_ref[...] = x_ref[...] + jnp.dot(mix_ref[...].astype(BF16), wb_ref[...], preferred_element_type=F32)


def _resid_matmul_cast(x, mix, w_f32, layer, *, tn):
    m, d = x.shape
    kdim = mix.shape[1]
    return pl.pallas_call(
        _resid_matmul_cast_kernel,
        out_shape=(jax.ShapeDtypeStruct((m, d), F32), jax.ShapeDtypeStruct((kdim, d), BF16)),
        grid=(d // tn,),
        in_specs=[pl.BlockSpec((m, tn), lambda j: (0, j)),
                  pl.BlockSpec((m, kdim), lambda j: (0, 0)),
                  pl.BlockSpec((None, kdim, tn), lambda j: (layer, 0, j))],
        out_specs=(pl.BlockSpec((m, tn), lambda j: (0, j)),
                   pl.BlockSpec((kdim, tn), lambda j: (0, j))),
        compiler_params=_params("parallel"),
        name="sample_out_proj",
    )(x, mix, w_f32)


def kernel(x_prompt, x_sample, state_pool, cache_k_win, cache_v_win, norm_g, w_in, pool_w, pool_scale,
           w_out, final_norm_g):
    batch, seq, d_model = x_prompt.shape
    bd, dec_seq, _ = x_sample.shape
    depth = norm_g.shape[0]
    d_pool = pool_scale.shape[1]
    d_attn = w_out.shape[1] - d_pool
    n_heads = d_attn // HEAD_DIM
    l_win = cache_k_win.shape[2]
    assert dec_seq == 1 and l_win == WIN_MAX and PAST_LEN + 1 >= POOL_MAX
    assert seq % (BAND * max(d for _, d in DILATED_PATTERNS)) == 0 and seq >= WIN_MAX
    q_off, k_off, v_off, ga_off = (2 * d_pool, 2 * d_pool + d_attn, 2 * d_pool + 2 * d_attn,
                                   2 * d_pool + 3 * d_attn)

    tiles = _tile_plan(seq, d_model, d_attn)
    pool_w_b = pool_w.astype(BF16)
    gains = norm_g.reshape(depth, 1, d_model)
    scales = pool_scale.reshape(depth, 1, d_pool)
    bias_p = _prompt_bias_table(n_heads)
    bias_s = _sample_bias_table(n_heads, l_win)

    kt_cache = cache_k_win.transpose(0, 1, 3, 4, 2)
    vt_cache = cache_v_win.transpose(0, 1, 3, 4, 2)
    xs = x_sample.reshape(bd, d_model)
    pool_s, k_s, v_s, w_in_b, w_out_b = [], [], [], [], []
    for l in range(depth):
        z, wb = _in_proj_cast(xs, gains, w_in, l, tn=tiles["sample_cols"])
        w_in_b.append(wb)
        mix, st = _sample_mix(z, state_pool[l], kt_cache, vt_cache, l, bias_s, pool_w_b, scales,
                              d_pool=d_pool, d_attn=d_attn, pos0=PAST_LEN)
        xs, wb = _resid_matmul_cast(xs, mix.reshape(bd, -1), w_out, l, tn=tiles["sample_cols"])
        w_out_b.append(wb)
        pool_s.append(st)
        k_s.append(z[:, k_off:k_off + d_attn].reshape(bd, 1, n_heads, HEAD_DIM))
        v_s.append(z[:, v_off:v_off + d_attn].reshape(bd, 1, n_heads, HEAD_DIM))
    y_sample = _rms_norm(xs, final_norm_g, tm=bd).reshape(bd, 1, d_model)

    keep = min(WIN_MAX, seq)
    xp = x_prompt.reshape(batch * seq, d_model)
    pool_p, windows = [], None
    for l in range(depth):
        z, *windows = _in_proj_window(xp, gains, w_in_b[l], l, depth, windows, seq=seq, keep=keep,
                                      k_off=k_off, v_off=v_off, d_attn=d_attn, tm=tiles["in_proj_rows"])
        a = _attn_prompt(z, bias_p, batch=batch, seq=seq, q_off=q_off, k_off=k_off, v_off=v_off, d_attn=d_attn)
        xp = _out_proj(xp, z, a, pool_w_b, scales, w_out_b[l], l, final_norm_g, seq=seq, tm=tiles["out_proj_rows"],
                       d_pool=d_pool, ga_off=ga_off, final_norm=(l == depth - 1))
        pool_p.append(z.reshape(batch, seq, -1)[:, seq - (POOL_MAX - 1):, :d_pool])
    y_prompt = xp.reshape(batch, seq, d_model)

    def from_slabs(slabs):
        return slabs.reshape(depth, batch, n_heads, HEAD_DIM, keep).transpose(0, 1, 4, 2, 3)

    return (y_prompt, y_sample,
            jnp.stack(pool_p), from_slabs(windows[0]), from_slabs(windows[1]),
            jnp.stack(pool_s), jnp.stack(k_s), jnp.stack(v_s))
```

```python
import functools

import numpy as np
import jax
import jax.numpy as jnp
from jax import lax
from jax.experimental import pallas as pl
from jax.experimental.pallas import tpu as pltpu

HEAD_DIM = 64
LANES = 128
SUBLANES = 8
HEADS_PER_TILE = LANES // HEAD_DIM
POOL_WINDOWS = (2, 4, 8, 16)
POOL_MAX = max(POOL_WINDOWS)
DILATED_PATTERNS = ((128, 1), (512, 4), (2048, 16))
N_PATTERNS = len(DILATED_PATTERNS)
BAND = 128
WIN_MAX = max(w for w, _ in DILATED_PATTERNS)
PAST_LEN = 16384
RMS_EPS = 1e-6
VMEM_LIMIT_BYTES = 56 * 2**20

F32 = jnp.float32
BF16 = jnp.bfloat16
NT_DIMS = (((1,), (1,)), ((), ()))


def _alibi_slopes(n_heads):
    return (2.0 ** (-8.0 * np.arange(1, n_heads + 1) / n_heads)).astype(np.float32)


def _silu(x):
    return x * (1.0 / (1.0 + jnp.exp(-x)))


def _params(*sem):
    return pltpu.CompilerParams(dimension_semantics=sem, vmem_limit_bytes=VMEM_LIMIT_BYTES)


def _tile_plan(seq, d_model, d_attn):
    in_proj_rows = min(1024, seq)
    out_proj_rows = min(512, seq)
    in_proj_bytes = (2 * 4 + 2) * in_proj_rows * d_model + 2 * 2 * d_model * d_attn + 3 * 2 * 4 * in_proj_rows * d_attn
    out_proj_bytes = 2 * d_model * d_model + (2 * 2 + 4 * 2 // 2) * 4 * out_proj_rows * d_model
    assert max(in_proj_bytes, out_proj_bytes) <= VMEM_LIMIT_BYTES
    return dict(in_proj_rows=in_proj_rows, out_proj_rows=out_proj_rows, sample_cols=d_attn)


NORM_ROWS = 256


def _normed_bf16(x_ref, g_ref, h_ref):
    x = x_ref[...]
    y = x * lax.rsqrt(jnp.mean(x * x, axis=-1, keepdims=True) + RMS_EPS)
    h_ref[...] = (y * g_ref[...]).astype(BF16)


def _in_proj_cast_kernel(x_ref, g_ref, w_ref, z_ref, wb_ref, h_ref):
    @pl.when(pl.program_id(0) == 0)
    def _():
        _normed_bf16(x_ref, g_ref, h_ref)

    wb_ref[...] = w_ref[...].astype(BF16)
    z_ref[...] = jnp.dot(h_ref[...], wb_ref[...], preferred_element_type=F32)


def _in_proj_window_kernel(x_ref, g_ref, w_ref, *rest, tiles_per_seq, first_win_tile, k_tile, v_tile,
                           zero_fill):
    z_ref, kt_ref, vt_ref, h_ref = rest[-4:]
    j = pl.program_id(1)
    tm = x_ref.shape[0]

    @pl.when(j == 0)
    def _():
        for row in range(0, tm, NORM_ROWS):
            rows = pl.ds(row, NORM_ROWS)
            x = x_ref[rows, :]
            y = x * lax.rsqrt(jnp.mean(x * x, axis=-1, keepdims=True) + RMS_EPS)
            h_ref[rows, :] = (y * g_ref[...]).astype(BF16)
            z_ref[rows, :] = jnp.dot(h_ref[rows, :], w_ref[...], preferred_element_type=F32)

    in_window = pl.program_id(0) % tiles_per_seq >= first_win_tile
    assert k_tile > 0 and v_tile > 0
    to_window = jnp.logical_and(in_window, jnp.logical_or(j == k_tile, j == v_tile))

    @pl.when(jnp.logical_and(j > 0, jnp.logical_not(to_window)))
    def _():
        z_ref[...] = jnp.dot(h_ref[...], w_ref[...], preferred_element_type=F32)

    for tile_j, win_ref in ((k_tile, kt_ref), (v_tile, vt_ref)):
        @pl.when(jnp.logical_and(in_window, j == tile_j))
        def _(win_ref=win_ref):
            for row in range(0, tm, NORM_ROWS):
                rows = pl.ds(row, NORM_ROWS)
                zc = jnp.dot(h_ref[rows, :], w_ref[...], preferred_element_type=F32)
                z_ref[rows, :] = zc
                win_ref[:, rows] = zc.T

    if zero_fill:
        @pl.when(jnp.logical_not(in_window))
        def _():
            kt_ref[...] = jnp.zeros_like(kt_ref)
            vt_ref[...] = jnp.zeros_like(vt_ref)


def _in_proj_cast(x, g, w_f32, layer, *, tn):
    m, d = x.shape
    n = w_f32.shape[2]
    return pl.pallas_call(
        _in_proj_cast_kernel,
        out_shape=(jax.ShapeDtypeStruct((m, n), F32), jax.ShapeDtypeStruct((d, n), BF16)),
        grid=(n // tn,),
        in_specs=[
            pl.BlockSpec((m, d), lambda j: (0, 0)),
            pl.BlockSpec((None, 1, d), lambda j: (layer, 0, 0)),
            pl.BlockSpec((None, d, tn), lambda j: (layer, 0, j)),
        ],
        out_specs=(pl.BlockSpec((m, tn), lambda j: (0, j)),
                   pl.BlockSpec((d, tn), lambda j: (0, j))),
        scratch_shapes=[pltpu.VMEM((m, d), BF16)],
        compiler_params=_params("arbitrary"),
        name="in_proj_sample",
    )(x, g, w_f32)


def _in_proj_window(x, g, w_bf16, layer, depth, windows, *, seq, keep, k_off, v_off, d_attn, tm):
    m, d = x.shape
    n = w_bf16.shape[1]
    tn = d_attn
    assert k_off % tn == 0 and v_off % tn == 0 and (seq - keep) % tm == 0 and seq % tm == 0
    batch = m // seq
    tiles_per_seq = seq // tm
    first_win_tile = (seq - keep) // tm
    win_tiles = tiles_per_seq - first_win_tile
    n_col = n // tn
    create = windows is None
    zero_fill = create and depth > 1
    n_slots = batch * first_win_tile * n_col
    n_zero = (depth - 1) * batch * win_tiles
    assert (layer == 0) == create and (not zero_fill or n_slots >= n_zero)

    def win_map(i, j):
        b, t = i // tiles_per_seq, i % tiles_per_seq
        own = (layer, b, 0, jnp.maximum(t - first_win_tile, 0))
        if not zero_fill:
            return own
        zero_block = ((b * first_win_tile + t) * n_col + j) * n_zero // n_slots
        other = (1 + zero_block // (batch * win_tiles), (zero_block // win_tiles) % batch, 0,
                 zero_block % win_tiles)
        return tuple(jnp.where(t >= first_win_tile, o, z) for o, z in zip(own, other))

    win = jax.ShapeDtypeStruct((depth, batch, d_attn, keep), F32)
    win_spec = pl.BlockSpec((None, None, d_attn, tm), win_map)
    extra_specs = [] if create else [pl.BlockSpec(memory_space=pl.ANY)] * 2
    return pl.pallas_call(
        functools.partial(_in_proj_window_kernel, tiles_per_seq=tiles_per_seq, first_win_tile=first_win_tile,
                          k_tile=k_off // tn, v_tile=v_off // tn, zero_fill=zero_fill),
        out_shape=(jax.ShapeDtypeStruct((m, n), F32), win, win),
        grid=(m // tm, n_col),
        in_specs=[
            pl.BlockSpec((tm, d), lambda i, j: (i, 0)),
            pl.BlockSpec((None, 1, d), lambda i, j: (layer, 0, 0)),
            pl.BlockSpec((d, tn), lambda i, j: (0, j)),
            *extra_specs,
        ],
        out_specs=(pl.BlockSpec((tm, tn), lambda i, j: (i, j)), win_spec, win_spec),
        scratch_shapes=[pltpu.VMEM((tm, d), BF16)],
        input_output_aliases={} if create else {3: 1, 4: 2},
        compiler_params=_params("arbitrary", "arbitrary"),
        name="in_proj_prompt",
    )(x, g, w_bf16, *(() if create else windows))


def _rms_norm_kernel(x_ref, g_ref, o_ref):
    x = x_ref[...]
    y = x * lax.rsqrt(jnp.mean(x * x, axis=-1, keepdims=True) + RMS_EPS)
    o_ref[...] = y * g_ref[...]


def _rms_norm(x, g, *, tm):
    m, d = x.shape
    return pl.pallas_call(
        _rms_norm_kernel,
        out_shape=jax.ShapeDtypeStruct((m, d), F32),
        grid=(m // tm,),
        in_specs=[pl.BlockSpec((tm, d), lambda i: (i, 0)),
                  pl.BlockSpec((1, d), lambda i: (0, 0))],
        out_specs=pl.BlockSpec((tm, d), lambda i: (i, 0)),
        compiler_params=_params("parallel"),
        name="final_norm",
    )(x, g.reshape(1, d))


def _prompt_bias_table(n_heads):
    slopes = _alibi_slopes(n_heads)
    qi = np.arange(BAND)[:, None] + BAND
    kj = np.arange(2 * BAND)[None, :]
    dist = qi - kj
    valid = (dist >= 0) & (dist <= BAND)
    valid_first = valid & (kj >= BAND)
    table = np.empty((n_heads, N_PATTERNS, 2, BAND, 2 * BAND), np.float32)
    for h in range(n_heads):
        for p, (_, d) in enumerate(DILATED_PATTERNS):
            bias = -slopes[h] * (dist * d).astype(np.float32)
            table[h, p, 0] = np.where(valid, bias, -np.inf)
            table[h, p, 1] = np.where(valid_first, bias, -np.inf)
    table = table.reshape(n_heads // HEADS_PER_TILE, HEADS_PER_TILE, N_PATTERNS, 2, BAND, 2 * BAND)
    table = table.transpose(0, 2, 3, 1, 4, 5)
    return jnp.asarray(table.reshape(n_heads // HEADS_PER_TILE, N_PATTERNS, 2, HEADS_PER_TILE * BAND, 2 * BAND))


ATTN_GROUP = 4
SOFTMAX_ROWS = 32


PREP_ROWS = 256


def _attn_prompt_kernel(q_ref, k_ref, v_ref, bias_ref, a_ref,
                        o_scr, m_scr, d_scr, qd_scrs, kd_scrs, vd_scrs,
                        group_bufs, *, seq):
    lane = lax.broadcasted_iota(jnp.int32, (BAND, LANES), 1)
    head0 = lane < HEAD_DIM
    stacked = HEADS_PER_TILE * BAND
    scale = HEAD_DIM ** -0.5
    srcs = (q_ref, k_ref, v_ref)
    assert DILATED_PATTERNS[1][1] ** 2 == DILATED_PATTERNS[2][1] and DILATED_PATTERNS[0][1] == 1
    d_mid, d_big = DILATED_PATTERNS[1][1], DILATED_PATTERNS[2][1]

    def put(ti, pi, sub, row, val):
        n_sub = seq // DILATED_PATTERNS[pi][1]
        if ti == 0:
            qd_scrs[pi][pl.ds(sub * n_sub + row, PREP_ROWS), :] = (val * scale).astype(BF16)
        else:
            dst = (kd_scrs, vd_scrs)[ti - 1][pi]
            dst[pl.ds(sub * (BAND + n_sub) + BAND + row, PREP_ROWS), :] = val.astype(BF16)

    for pi, (_, d) in enumerate(DILATED_PATTERNS):
        n_sub = seq // d
        for dst in (kd_scrs[pi], vd_scrs[pi]):
            for r in range(d):
                dst[pl.ds(r * (BAND + n_sub), BAND), :] = jnp.zeros((BAND, LANES), BF16)

    def prep_dense(c, carry):
        row = pl.multiple_of(c * PREP_ROWS, PREP_ROWS)
        for ti in range(3):
            put(ti, 0, 0, row, srcs[ti][pl.ds(row, PREP_ROWS), :])
        return carry

    lax.fori_loop(0, seq // PREP_ROWS, prep_dense, 0)

    n_mid = seq // d_mid

    def prep_mid(c, carry):
        row = pl.multiple_of(c * PREP_ROWS, PREP_ROWS)
        for ti in range(3):
            for r in range(d_mid):
                val = srcs[ti][pl.ds(r + d_mid * row, PREP_ROWS, stride=d_mid), :]
                o_scr[ti, pl.ds(r * n_mid + row, PREP_ROWS), :] = val
                put(ti, 1, r, row, val)
        return carry

    lax.fori_loop(0, n_mid // PREP_ROWS, prep_mid, 0)

    n_big = seq // d_big
    for c in range(n_big // PREP_ROWS):
        for ti in range(3):
            for r in range(d_mid):
                for a in range(d_mid):
                    val = o_scr[ti, pl.ds(r * n_mid + a + d_mid * c * PREP_ROWS, PREP_ROWS, stride=d_mid), :]
                    put(ti, 2, r + d_mid * a, c * PREP_ROWS, val)

    def aligned(row):
        return row if isinstance(row, int) else pl.multiple_of(row, BAND)

    stages = []
    for pi, (_, d) in enumerate(DILATED_PATTERNS):
        nblk = seq // d // BAND
        n_sub = seq // d
        qd, kd, vd = qd_scrs[pi], kd_scrs[pi], vd_scrs[pi]

        def units_of(grp, d=d, nblk=nblk, n_sub=n_sub):
            units = []
            for g in range(ATTN_GROUP):
                idx = grp * ATTN_GROUP + g
                r = idx // nblk
                blk = idx % nblk
                q_row = aligned(r * n_sub + blk * BAND)
                k_row = aligned(r * (BAND + n_sub) + blk * BAND)
                if d == d_big:
                    out_rows = pl.ds((r % d_mid) * n_mid + r // d_mid + d_mid * BAND * blk, BAND, stride=d_mid)
                else:
                    out_rows = pl.ds(q_row, BAND)
                first = int(blk == 0) if isinstance(blk, int) else jnp.asarray(blk == 0, jnp.int32)
                units.append((q_row, k_row, out_rows, first))
            return units

        def scores(units, bufs, pi=pi, qd=qd, kd=kd):
            s_buf, _, mg_buf, _ = bufs
            for g, (q_row, k_row, _, first) in enumerate(units):
                q = qd[pl.ds(q_row, BAND), :]
                zero = jnp.zeros_like(q)
                q2 = jnp.concatenate([jnp.where(head0, q, zero), jnp.where(head0, zero, q)], axis=0)
                s = lax.dot_general(q2, kd[pl.ds(k_row, 2 * BAND), :], NT_DIMS,
                                    preferred_element_type=F32) + bias_ref[pi, first]
                s_buf[g] = s
                mg_buf[g] = jnp.broadcast_to(jnp.max(s, axis=-1, keepdims=True), (stacked, LANES))

        def softmax_values(units, bufs, pi=pi, vd=vd):
            s_buf, p_buf, mg_buf, dg_buf = bufs
            for g in range(ATTN_GROUP):
                for c in range(stacked // SOFTMAX_ROWS):
                    sl = pl.ds(c * SOFTMAX_ROWS, SOFTMAX_ROWS)
                    m = mg_buf[g, sl, :]
                    p = jnp.exp(s_buf[g, sl, :] - jnp.concatenate([m, m], axis=-1))
                    p_buf[g, sl, :] = p.astype(BF16)
                    dg_buf[g, sl, :] = jnp.broadcast_to(jnp.sum(p, axis=-1, keepdims=True), (SOFTMAX_ROWS, LANES))
            for g, (_, _, out_rows, _) in enumerate(units):
                m_scr[pi, out_rows, :] = jnp.where(head0, mg_buf[g, pl.ds(0, BAND), :], mg_buf[g, pl.ds(BAND, BAND), :])
                d_scr[pi, out_rows, :] = jnp.where(head0, dg_buf[g, pl.ds(0, BAND), :], dg_buf[g, pl.ds(BAND, BAND), :])
            for g, (_, k_row, out_rows, _) in enumerate(units):
                o2 = jnp.dot(p_buf[g], vd[pl.ds(k_row, 2 * BAND), :], preferred_element_type=F32)
                o_scr[pi, out_rows, :] = jnp.where(head0, o2[:BAND], o2[BAND:])

        stages.append((units_of, scores, softmax_values, d * nblk // ATTN_GROUP))

    def pair(j, units_of, scores, softmax_values, score_ahead):
        even, odd = units_of(2 * j), units_of(2 * j + 1)
        scores(odd, group_bufs[1])
        softmax_values(even, group_bufs[0])
        score_ahead()
        softmax_values(odd, group_bufs[1])

    stages[0][1](stages[0][0](0), group_bufs[0])
    for pi, (units_of, scores, softmax_values, n_groups) in enumerate(stages):
        def body(j, carry, units_of=units_of, scores=scores, softmax_values=softmax_values):
            pair(j, units_of, scores, softmax_values, lambda: scores(units_of(2 * j + 2), group_bufs[0]))
            return carry

        lax.fori_loop(0, n_groups // 2 - 1, body, 0)
        if pi + 1 < len(stages):
            next_units_of, next_scores = stages[pi + 1][:2]
            score_ahead = functools.partial(next_scores, next_units_of(0), group_bufs[0])
        else:
            score_ahead = lambda: None
        pair(n_groups // 2 - 1, units_of, scores, softmax_values, score_ahead)

    chunk = 2 * BAND

    def combine(c, carry):
        start = pl.multiple_of(c * chunk, chunk)
        sl = pl.ds(start, chunk)
        natural = pl.ds(start // n_mid + d_mid * (start % n_mid), chunk, stride=d_mid)
        ms = [m_scr[0, natural, :], m_scr[1, sl, :], m_scr[2, sl, :]]
        m = jnp.maximum(jnp.maximum(ms[0], ms[1]), ms[2])
        es = [jnp.exp(mi - m) for mi in ms]
        num = es[0] * o_scr[0, natural, :] + es[1] * o_scr[1, sl, :] + es[2] * o_scr[2, sl, :]
        den = es[0] * d_scr[0, natural, :] + es[1] * d_scr[1, sl, :] + es[2] * d_scr[2, sl, :]
        a_ref[natural, :] = num / den
        return carry

    lax.fori_loop(0, seq // chunk, combine, 0)


def _attn_prompt(z, bias, *, batch, seq, q_off, k_off, v_off, d_attn):
    n_pairs = d_attn // LANES
    stacked = HEADS_PER_TILE * BAND

    def col(off):
        return pl.BlockSpec((seq, LANES), lambda b, hp: (b, off // LANES + hp))

    return pl.pallas_call(
        functools.partial(_attn_prompt_kernel, seq=seq),
        out_shape=jax.ShapeDtypeStruct((batch * seq, d_attn), F32),
        grid=(batch, n_pairs),
        in_specs=[
            col(q_off), col(k_off), col(v_off),
            pl.BlockSpec((None, N_PATTERNS, 2, stacked, 2 * BAND), lambda b, hp: (hp, 0, 0, 0, 0)),
        ],
        out_specs=pl.BlockSpec((seq, LANES), lambda b, hp: (b, hp)),
        scratch_shapes=[pltpu.VMEM((N_PATTERNS, seq, LANES), F32),
                        pltpu.VMEM((N_PATTERNS, seq, LANES), F32),
                        pltpu.VMEM((N_PATTERNS, seq, LANES), F32),
                        [pltpu.VMEM((seq, LANES), BF16) for _ in DILATED_PATTERNS],
                        [pltpu.VMEM((seq + d * BAND, LANES), BF16) for _, d in DILATED_PATTERNS],
                        [pltpu.VMEM((seq + d * BAND, LANES), BF16) for _, d in DILATED_PATTERNS],
                        [[pltpu.VMEM((ATTN_GROUP, stacked, 2 * BAND), F32),
                          pltpu.VMEM((ATTN_GROUP, stacked, 2 * BAND), BF16),
                          pltpu.VMEM((ATTN_GROUP, stacked, LANES), F32),
                          pltpu.VMEM((ATTN_GROUP, stacked, LANES), F32)]
                         for _ in range(2)]],
        compiler_params=_params("parallel", "parallel"),
        name="attn_prompt",
    )(z, z, z, bias)


def _pooled_window(ext_ref, row0, n_rows, cols, w, pos):
    cur = ext_ref[pl.ds(row0, n_rows), cols]
    acc = cur
    for i in range(1, w):
        acc = acc + ext_ref[pl.ds(row0 - i, n_rows), cols]
    return acc / jnp.minimum(w, pos + 1).astype(F32) - cur


POOL_ROWS = 64
MIX_ROWS = 128


def _prompt_mix(u_ref, halo_ref, gp_ref, ga_ref, a_ref, pw_ref, ps_ref, ext_ref, level_refs, pooled_ref, mix_ref,
                project, *, tile, tm, d_pool):
    halo = halo_ref.shape[0]
    n_grp = len(POOL_WINDOWS)
    group = d_pool // n_grp
    assert all(w == 2 << g for g, w in enumerate(POOL_WINDOWS)) and halo >= (n_grp) * SUBLANES
    ext_ref[pl.ds(0, halo), :] = jnp.where(tile == 0, 0.0, halo_ref[...])
    ext_ref[pl.ds(halo, tm), :] = u_ref[...]
    for chunk in range(0, tm, MIX_ROWS):
        rows = pl.ds(chunk, MIX_ROWS)
        below, below_cols = ext_ref, slice(0, d_pool)
        for g, w in enumerate(POOL_WINDOWS):
            shift = w // 2
            level = level_refs[g] if g + 1 < n_grp else None
            own_cols = slice(g * group, (g + 1) * group)

            def window_sum(row, n, below=below, below_cols=below_cols, shift=shift):
                return below[pl.ds(row, n), below_cols] + below[pl.ds(row - shift, n), below_cols]

            first = (g + 1) * SUBLANES
            if chunk == 0 and level is not None and first < halo:
                level[pl.ds(first, halo - first), :] = window_sum(first, halo - first)
            for row in range(chunk, chunk + MIX_ROWS, POOL_ROWS):
                s = window_sum(halo + row, POOL_ROWS)
                if level is not None:
                    level[pl.ds(halo + row, POOL_ROWS), :] = s
                pos = tile * tm + row + lax.broadcasted_iota(jnp.int32, (POOL_ROWS, 1), 0)
                cnt = jnp.minimum(w, pos + 1).astype(F32)
                pooled = s[:, :group] / cnt - ext_ref[pl.ds(halo + row, POOL_ROWS), own_cols]
                pooled_ref[pl.ds(row, POOL_ROWS), own_cols] = pooled.astype(BF16)
            mixed = jnp.dot(pooled_ref[rows, own_cols], pw_ref[g], preferred_element_type=F32) * ps_ref[:, own_cols]
            mix_ref[rows, own_cols] = (mixed * _silu(gp_ref[rows, own_cols])).astype(BF16)
            if level is not None:
                below, below_cols = level, slice(group, d_pool - g * group)
        mix_ref[rows, d_pool:] = (a_ref[rows, :] * _silu(ga_ref[rows, :])).astype(BF16)
        project(rows)


def _out_proj_kernel(x_ref, u_ref, halo_ref, gp_ref, ga_ref, a_ref, pw_ref, ps_ref, wo_ref, gf_ref,
                     o_ref, ext_ref, level_refs, pooled_ref, mix_ref, *, tm, tiles_per_seq, d_pool, final_norm):
    tile = pl.program_id(0) % tiles_per_seq

    def project(rows):
        acc = x_ref[rows, :] + jnp.dot(mix_ref[rows, :], wo_ref[...], preferred_element_type=F32)
        if final_norm:
            acc = acc * lax.rsqrt(jnp.mean(acc * acc, axis=-1, keepdims=True) + RMS_EPS) * gf_ref[...]
        o_ref[rows, :] = acc

    _prompt_mix(u_ref, halo_ref, gp_ref, ga_ref, a_ref, pw_ref, ps_ref, ext_ref, level_refs, pooled_ref, mix_ref,
                project, tile=tile, tm=tm, d_pool=d_pool)


def _out_proj(x, z, a, pool_w_bf16, pool_scale, w_out_bf16, layer, final_g, *, seq, tm, d_pool, ga_off,
              final_norm):
    m, d = x.shape
    n_grp = len(POOL_WINDOWS)
    halo = n_grp * SUBLANES
    grp = d_pool // n_grp
    d_attn = a.shape[1]
    assert ga_off % d_attn == 0 and d_pool == d_attn and halo >= POOL_MAX - 1 and tm % POOL_ROWS == 0
    return pl.pallas_call(
        functools.partial(_out_proj_kernel, tm=tm, tiles_per_seq=seq // tm, d_pool=d_pool,
                          final_norm=final_norm),
        out_shape=jax.ShapeDtypeStruct((m, d), F32),
        grid=(m // tm,),
        in_specs=[
            pl.BlockSpec((tm, d), lambda i: (i, 0)),
            pl.BlockSpec((tm, d_pool), lambda i: (i, 0)),
            pl.BlockSpec((halo, d_pool), lambda i: (jnp.maximum(i * (tm // halo) - 1, 0), 0)),
            pl.BlockSpec((tm, d_pool), lambda i: (i, 1)),
            pl.BlockSpec((tm, d_attn), lambda i: (i, ga_off // d_attn)),
            pl.BlockSpec((tm, d_attn), lambda i: (i, 0)),
            pl.BlockSpec((None, n_grp, grp, grp), lambda i: (layer, 0, 0, 0)),
            pl.BlockSpec((None, 1, d_pool), lambda i: (layer, 0, 0)),
            pl.BlockSpec((d_pool + d_attn, d), lambda i: (0, 0), pipeline_mode=pl.Buffered(1)),
            pl.BlockSpec((1, d), lambda i: (0, 0)),
        ],
        out_specs=pl.BlockSpec((tm, d), lambda i: (i, 0)),
        scratch_shapes=[pltpu.VMEM((halo + tm, d_pool), F32),
                        [pltpu.VMEM((halo + tm, d_pool - g * grp), F32) for g in range(n_grp - 1)],
                        pltpu.VMEM((tm, d_pool), BF16),
                        pltpu.VMEM((tm, d_pool + d_attn), BF16)],
        compiler_params=_params("parallel"),
        name="out_proj",
    )(x, z, z, z, z, a, pool_w_bf16, pool_scale, w_out_bf16, final_g.reshape(1, d))


def _sample_bias_table(n_heads, l_win):
    slopes = _alibi_slopes(n_heads)
    dist = l_win - np.arange(l_win)
    table = np.empty((n_heads, SUBLANES, l_win), np.float32)
    for p, (window, d) in enumerate(DILATED_PATTERNS):
        valid = (dist % d == 0) & (dist <= window)
        table[:, p, :] = np.where(valid[None, :], -slopes[:, None] * dist[None, :].astype(np.float32), -np.inf)
    table[:, N_PATTERNS:, :] = table[:, :1, :]
    return jnp.asarray(table)


def _sample_mix_kernel(z_ref, st_ref, kt_ref, vt_ref, bias_ref, pw_ref, ps_ref,
                       mix_ref, st_out_ref, ext_ref, a_scr, *, d_pool, d_attn, pos0):
    n_heads = d_attn // HEAD_DIM
    u = z_ref[:, 0:d_pool]
    gp = z_ref[:, d_pool:2 * d_pool]
    ga = z_ref[:, 2 * d_pool + 3 * d_attn:2 * d_pool + 4 * d_attn]
    q_off, k_off, v_off = 2 * d_pool, 2 * d_pool + d_attn, 2 * d_pool + 2 * d_attn

    n_state = POOL_MAX - 1
    ext_ref[pl.ds(0, 1), :] = jnp.zeros((1, d_pool), F32)
    ext_ref[pl.ds(1, n_state), :] = st_ref[...]
    ext_ref[pl.ds(POOL_MAX, 1), :] = u
    pos = jnp.full((1, 1), pos0, jnp.int32)
    group = d_pool // len(POOL_WINDOWS)
    parts = []
    for g, w in enumerate(POOL_WINDOWS):
        cols = slice(g * group, (g + 1) * group)
        pooled = _pooled_window(ext_ref, POOL_MAX, 1, cols, w, pos)
        parts.append(jnp.dot(pooled.astype(BF16), pw_ref[g], preferred_element_type=F32))
    pm = jnp.concatenate(parts, axis=-1) * ps_ref[...]
    st_out_ref[pl.ds(0, n_state - 1), :] = st_ref[pl.ds(1, n_state - 1), :]
    st_out_ref[pl.ds(n_state - 1, 1), :] = u

    is_pattern = lax.broadcasted_iota(jnp.int32, (SUBLANES, 1), 0) < N_PATTERNS
    for h in range(n_heads):
        cols = slice(h * HEAD_DIM, (h + 1) * HEAD_DIM)

        def head_row(off, cols=cols):
            return z_ref[:, off:off + d_attn][:, cols]

        q8 = jnp.broadcast_to(head_row(q_off) * (HEAD_DIM ** -0.5), (SUBLANES, HEAD_DIM)).astype(BF16)
        k_new = head_row(k_off).astype(BF16).astype(F32)
        v_new = head_row(v_off).astype(BF16).astype(F32)
        s = jnp.dot(q8, kt_ref[h].astype(BF16), preferred_element_type=F32) + bias_ref[h]
        s_new = jnp.sum(q8.astype(F32) * k_new, axis=-1, keepdims=True)
        m = jnp.maximum(jnp.max(s, axis=-1, keepdims=True), s_new)
        p = jnp.exp(s - m)
        p_new = jnp.exp(s_new - m)
        den = jnp.sum(p, axis=-1, keepdims=True) + p_new
        o = lax.dot_general(p.astype(BF16), vt_ref[h].astype(BF16), NT_DIMS, preferred_element_type=F32)
        o = (o + p_new.astype(BF16).astype(F32) * v_new) / den
        lse = jnp.where(is_pattern, m + jnp.log(den), -jnp.inf)
        e = jnp.exp(lse - jnp.max(lse, axis=0, keepdims=True))
        a_scr[:, cols] = jnp.sum(e * o, axis=0, keepdims=True) / jnp.sum(e, axis=0, keepdims=True)

    mix_ref[:, :d_pool] = pm * _silu(gp)
    mix_ref[:, d_pool:] = a_scr[...] * _silu(ga)


def _sample_mix(z, state, kt_cache, vt_cache, layer, bias, pool_w_bf16, pool_scale, *, d_pool, d_attn, pos0):
    bd, n_state, _ = state.shape
    n_heads, _, l_win = kt_cache.shape[2:]
    n_in = z.shape[1]
    n_grp = len(POOL_WINDOWS)
    grp = d_pool // n_grp
    cache_spec = pl.BlockSpec((None, None, n_heads, HEAD_DIM, l_win), lambda b: (layer, b, 0, 0, 0))
    return pl.pallas_call(
        functools.partial(_sample_mix_kernel, d_pool=d_pool, d_attn=d_attn, pos0=pos0),
        out_shape=(jax.ShapeDtypeStruct((bd, 1, d_pool + d_attn), F32),
                   jax.ShapeDtypeStruct((bd, n_state, d_pool), F32)),
        grid=(bd,),
        in_specs=[
            pl.BlockSpec((None, 1, n_in), lambda b: (b, 0, 0)),
            pl.BlockSpec((None, n_state, d_pool), lambda b: (b, 0, 0)),
            cache_spec, cache_spec,
            pl.BlockSpec((n_heads, SUBLANES, l_win), lambda b: (0, 0, 0)),
            pl.BlockSpec((None, n_grp, grp, grp), lambda b: (layer, 0, 0, 0)),
            pl.BlockSpec((None, 1, d_pool), lambda b: (layer, 0, 0)),
        ],
        out_specs=(pl.BlockSpec((None, 1, d_pool + d_attn), lambda b: (b, 0, 0)),
                   pl.BlockSpec((None, n_state, d_pool), lambda b: (b, 0, 0))),
        scratch_shapes=[pltpu.VMEM((POOL_MAX + SUBLANES, d_pool), F32),
                        pltpu.VMEM((1, d_attn), F32)],
        compiler_params=_params("parallel"),
        name="sample_mix",
    )(z.reshape(bd, 1, n_in), state, kt_cache, vt_cache, bias, pool_w_bf16, pool_scale)


def _resid_matmul_cast_kernel(x_ref, mix_ref, w_ref, o_ref, wb_ref):
    wb_ref[...] = w_ref[...].astype(BF16)
    o_ref[...] = x_ref[...] + jnp.dot(mix_ref[...].astype(BF16), wb_ref[...], preferred_element_type=F32)


def _resid_matmul_cast(x, mix, w_f32, layer, *, tn):
    m, d = x.shape
    kdim = mix.shape[1]
    return pl.pallas_call(
        _resid_matmul_cast_kernel,
        out_shape=(jax.ShapeDtypeStruct((m, d), F32), jax.ShapeDtypeStruct((kdim, d), BF16)),
        grid=(d // tn,),
        in_specs=[pl.BlockSpec((m, tn), lambda j: (0, j)),
                  pl.BlockSpec((m, kdim), lambda j: (0, 0)),
                  pl.BlockSpec((None, kdim, tn), lambda j: (layer, 0, j))],
        out_specs=(pl.BlockSpec((m, tn), lambda j: (0, j)),
                   pl.BlockSpec((kdim, tn), lambda j: (0, j))),
        compiler_params=_params("parallel"),
        name="sample_out_proj",
    )(x, mix, w_f32)


def kernel(x_prompt, x_sample, state_pool, cache_k_win, cache_v_win, norm_g, w_in, pool_w, pool_scale,
           w_out, final_norm_g):
    batch, seq, d_model = x_prompt.shape
    bd, dec_seq, _ = x_sample.shape
    depth = norm_g.shape[0]
    d_pool = pool_scale.shape[1]
    d_attn = w_out.shape[1] - d_pool
    n_heads = d_attn // HEAD_DIM
    l_win = cache_k_win.shape[2]
    assert dec_seq == 1 and l_win == WIN_MAX and PAST_LEN + 1 >= POOL_MAX
    assert seq % (BAND * max(d for _, d in DILATED_PATTERNS)) == 0 and seq >= WIN_MAX
    q_off, k_off, v_off, ga_off = (2 * d_pool, 2 * d_pool + d_attn, 2 * d_pool + 2 * d_attn,
                                   2 * d_pool + 3 * d_attn)

    tiles = _tile_plan(seq, d_model, d_attn)
    pool_w_b = pool_w.astype(BF16)
    gains = norm_g.reshape(depth, 1, d_model)
    scales = pool_scale.reshape(depth, 1, d_pool)
    bias_p = _prompt_bias_table(n_heads)
    bias_s = _sample_bias_table(n_heads, l_win)

    kt_cache = cache_k_win.transpose(0, 1, 3, 4, 2)
    vt_cache = cache_v_win.transpose(0, 1, 3, 4, 2)
    xs = x_sample.reshape(bd, d_model)
    pool_s, k_s, v_s, w_in_b, w_out_b = [], [], [], [], []
    for l in range(depth):
        z, wb = _in_proj_cast(xs, gains, w_in, l, tn=tiles["sample_cols"])
        w_in_b.append(wb)
        mix, st = _sample_mix(z, state_pool[l], kt_cache, vt_cache, l, bias_s, pool_w_b, scales,
                              d_pool=d_pool, d_attn=d_attn, pos0=PAST_LEN)
        xs, wb = _resid_matmul_cast(xs, mix.reshape(bd, -1), w_out, l, tn=tiles["sample_cols"])
        w_out_b.append(wb)
        pool_s.append(st)
        k_s.append(z[:, k_off:k_off + d_attn].reshape(bd, 1, n_heads, HEAD_DIM))
        v_s.append(z[:, v_off:v_off + d_attn].reshape(bd, 1, n_heads, HEAD_DIM))
    y_sample = _rms_norm(xs, final_norm_g, tm=bd).reshape(bd, 1, d_model)

    keep = min(WIN_MAX, seq)
    xp = x_prompt.reshape(batch * seq, d_model)
    pool_p, windows = [], None
    for l in range(depth):
        z, *windows = _in_proj_window(xp, gains, w_in_b[l], l, depth, windows, seq=seq, keep=keep,
                                      k_off=k_off, v_off=v_off, d_attn=d_attn, tm=tiles["in_proj_rows"])
        a = _attn_prompt(z, bias_p, batch=batch, seq=seq, q_off=q_off, k_off=k_off, v_off=v_off, d_attn=d_attn)
        xp = _out_proj(xp, z, a, pool_w_b, scales, w_out_b[l], l, final_norm_g, seq=seq, tm=tiles["out_proj_rows"],
                       d_pool=d_pool, ga_off=ga_off, final_norm=(l == depth - 1))
        pool_p.append(z.reshape(batch, seq, -1)[:, seq - (POOL_MAX - 1):, :d_pool])
    y_prompt = xp.reshape(batch, seq, d_model)

    def from_slabs(slabs):
        return slabs.reshape(depth, batch, n_heads, HEAD_DIM, keep).transpose(0, 1, 4, 2, 3)

    return (y_prompt, y_sample,
            jnp.stack(pool_p), from_slabs(windows[0]), from_slabs(windows[1]),
            jnp.stack(pool_s), jnp.stack(k_s), jnp.stack(v_s))
```
